```python
import jax, jax.numpy as jnp
from jax import lax
import numpy as np

D_MODEL = 1024
BATCH = 8
SEQ = 4096
DEPTH = 4

CONV_WIDTH_A = 512
CONV_K = 3
MLA_HEADS = 8
MLA_NOPE = 64
MLA_ROPE = 32
MLA_V = 64
MLA_Q_RANK = 384
MLA_KV_RANK = 256
ROPE_THETA = 10000.0
ATTN_BLOCK = 128
GLA_HEADS = 4
GLA_DK = 128
GLA_DV = 256
GLA_GATE_RANK = 16
GLA_GATE_TAU = 16.0
GLA_CHUNK = 64
D_FF = 4 * D_MODEL
EPS = 1e-6

N_EVEN = (DEPTH + 1) // 2
N_ODD = DEPTH // 2
EVEN_SPLITS = (CONV_WIDTH_A, CONV_WIDTH_A, CONV_WIDTH_A, MLA_Q_RANK, MLA_KV_RANK, MLA_ROPE)
ODD_SPLITS = (GLA_HEADS * GLA_DK, GLA_HEADS * GLA_DK, GLA_HEADS * GLA_DV, GLA_HEADS * GLA_DV, GLA_GATE_RANK)
EVEN_IN = sum(EVEN_SPLITS)
ODD_IN = sum(ODD_SPLITS)
EVEN_MIX = CONV_WIDTH_A + MLA_HEADS * MLA_V

kernel_name = "hybrid_shortconv_mla_gla_sqrelu"


def rms_norm(x, g):
    xf = x.astype(jnp.float32)
    y = xf * lax.rsqrt(jnp.mean(xf * xf, axis=-1, keepdims=True) + EPS)
    return (y * g.astype(jnp.float32)).astype(x.dtype)


def split_cols(z, sizes):
    idx = [int(i) for i in np.cumsum(sizes)[:-1]]
    return jnp.split(z, idx, axis=-1)


def apply_rope(t, cos, sin):
    half = t.shape[-1] // 2
    t1, t2 = t[..., :half], t[..., half:]
    return jnp.concatenate([t1 * cos - t2 * sin, t2 * cos + t1 * sin], axis=-1)


def causal_short_conv(u, w):
    c = u.shape[-1]
    return lax.conv_general_dilated(
        u, w[:, None, :].astype(u.dtype), window_strides=(1,),
        padding=[(CONV_K - 1, 0)], dimension_numbers=("NWC", "WIO", "NWC"),
        feature_group_count=c)


def causal_attention_blocks(q, k, v):
    b, s, h, dq = q.shape
    dv = v.shape[-1]
    nb = s // ATTN_BLOCK
    scale = dq ** -0.5
    kt = k.transpose(0, 2, 1, 3)
    vt = v.transpose(0, 2, 1, 3)
    qb = q.reshape(b, nb, ATTN_BLOCK, h, dq).transpose(1, 0, 3, 2, 4)
    kpos = jnp.arange(s)

    def one_block(args):
        qi, i = args
        sc = jnp.einsum("bhqd,bhkd->bhqk", qi, kt, preferred_element_type=jnp.float32) * scale
        qpos = i * ATTN_BLOCK + jnp.arange(ATTN_BLOCK)
        sc = jnp.where(kpos[None, :] <= qpos[:, None], sc, -jnp.inf)
        p = jax.nn.softmax(sc, axis=-1).astype(vt.dtype)
        return jnp.einsum("bhqk,bhkd->bhqd", p, vt)

    o = lax.map(one_block, (qb, jnp.arange(nb)))
    return o.transpose(1, 0, 3, 2, 4).reshape(b, s, h * dv)


def gla_chunked(q, k, v, log_a):
    b, s, h, dk = q.shape
    dv = v.shape[-1]
    c = GLA_CHUNK
    n = s // c

    def chunks(t):
        return t.reshape(b, n, c, h, t.shape[-1]).transpose(1, 0, 3, 2, 4).astype(jnp.float32)

    qc = chunks(q) * (dk ** -0.5)
    kc = chunks(k)
    vc = chunks(v)
    bc = jnp.cumsum(chunks(log_a), axis=3)
    b_last = bc[:, :, :, -1:, :]
    q_e = qc * jnp.exp(bc)
    k_e = kc * jnp.exp(-bc)
    k_l = kc * jnp.exp(b_last - bc)
    dec = jnp.exp(b_last[:, :, :, 0, :])
    causal = jnp.tril(jnp.ones((c, c), dtype=bool))
    attn = jnp.where(causal, jnp.einsum("nbhid,nbhjd->nbhij", q_e, k_e), 0.0)
    o_intra = jnp.einsum("nbhij,nbhjv->nbhiv", attn, vc)

    def step(state, inp):
        q_n, k_n, v_n, dec_n = inp
        o_n = jnp.einsum("bhid,bhdv->bhiv", q_n, state)
        state = dec_n[..., None] * state + jnp.einsum("bhjd,bhjv->bhdv", k_n, v_n)
        return state, o_n

    state0 = jnp.zeros((b, h, dk, dv), jnp.float32)
    _, o_inter = lax.scan(step, state0, (q_e, k_l, vc, dec))
    o = (o_intra + o_inter).transpose(1, 0, 3, 2, 4).reshape(b, s, h, dv)
    return o.astype(v.dtype)


def even_mixer(h, cos, sin, w_in, conv_w, q_norm_g, w_qb, kv_norm_g, w_kvb, w_out):
    b, s, _ = h.shape
    z = h @ w_in
    a_b, a_c, a_v, z_q, z_kv, z_pe = split_cols(z, EVEN_SPLITS)
    y_a = a_b * causal_short_conv(a_c * a_v, conv_w)
    q = (rms_norm(z_q, q_norm_g) @ w_qb).reshape(b, s, MLA_HEADS, MLA_NOPE + MLA_ROPE)
    q = jnp.concatenate([q[..., :MLA_NOPE],
                         apply_rope(q[..., MLA_NOPE:], cos[:, :, None, :], sin[:, :, None, :])], axis=-1)
    kv = (rms_norm(z_kv, kv_norm_g) @ w_kvb).reshape(b, s, MLA_HEADS, MLA_NOPE + MLA_V)
    k_pe = apply_rope(z_pe, cos, sin)
    k = jnp.concatenate([kv[..., :MLA_NOPE],
                         jnp.broadcast_to(k_pe[:, :, None, :], (b, s, MLA_HEADS, MLA_ROPE))], axis=-1)
    y_b = causal_attention_blocks(q, k, kv[..., MLA_NOPE:])
    return jnp.concatenate([y_a, y_b], axis=-1) @ w_out


def odd_mixer(h, w_in, w_gate2, b_gate2, o_norm_g, w_out):
    b, s, _ = h.shape
    z = h @ w_in
    q, k, v, g, g_low = split_cols(z, ODD_SPLITS)
    log_a = jax.nn.log_sigmoid((g_low @ w_gate2 + b_gate2).astype(jnp.float32)) / GLA_GATE_TAU
    o = gla_chunked(q.reshape(b, s, GLA_HEADS, GLA_DK), k.reshape(b, s, GLA_HEADS, GLA_DK),
                    v.reshape(b, s, GLA_HEADS, GLA_DV), log_a.reshape(b, s, GLA_HEADS, GLA_DK))
    o = rms_norm(o, o_norm_g).reshape(b, s, GLA_HEADS * GLA_DV)
    return (o * jax.nn.silu(g)) @ w_out


def setup_inputs(seed: int = 0) -> dict:
    key = jax.random.key(seed)
    ks = jax.random.split(key, 20)
    resid = (2.0 * DEPTH) ** -0.5

    def nrm(k, shape, fan_in, scale=1.0):
        return jax.random.normal(k, shape, jnp.float32) * (scale * fan_in ** -0.5)

    def gain(k, shape):
        return 1.0 + 0.02 * jax.random.normal(k, shape, jnp.float32)

    x = jax.random.normal(ks[0], (BATCH, SEQ, D_MODEL), jnp.float32)
    offsets = jax.random.randint(ks[1], (BATCH, 1), 0, 4096, dtype=jnp.int32)
    positions = (offsets + jnp.arange(SEQ, dtype=jnp.int32)[None, :]).astype(jnp.int32)
    return {
        "x": x,
        "positions": positions,
        "mix_norm_g": gain(ks[2], (DEPTH, D_MODEL)),
        "mlp_norm_g": gain(ks[3], (DEPTH, D_MODEL)),
        "final_norm_g": gain(ks[4], (D_MODEL,)),
        "ev_w_in": nrm(ks[5], (N_EVEN, D_MODEL, EVEN_IN), D_MODEL),
        "ev_conv_w": nrm(ks[6], (N_EVEN, CONV_K, CONV_WIDTH_A), CONV_K),
        "ev_q_norm_g": gain(ks[7], (N_EVEN, MLA_Q_RANK)),
        "ev_w_qb": nrm(ks[8], (N_EVEN, MLA_Q_RANK, MLA_HEADS * (MLA_NOPE + MLA_ROPE)), MLA_Q_RANK),
        "ev_kv_norm_g": gain(ks[9], (N_EVEN, MLA_KV_RANK)),
        "ev_w_kvb": nrm(ks[10], (N_EVEN, MLA_KV_RANK, MLA_HEADS * (MLA_NOPE + MLA_V)), MLA_KV_RANK),
        "ev_w_out": nrm(ks[11], (N_EVEN, EVEN_MIX, D_MODEL), EVEN_MIX, resid),
        "od_w_in": nrm(ks[12], (N_ODD, D_MODEL, ODD_IN), D_MODEL),
        "od_w_gate2": nrm(ks[13], (N_ODD, GLA_GATE_RANK, GLA_HEADS * GLA_DK), GLA_GATE_RANK),
        "od_b_gate2": 0.1 * jax.random.normal(ks[14], (N_ODD, GLA_HEADS * GLA_DK), jnp.float32),
        "od_o_norm_g": gain(ks[15], (N_ODD, GLA_DV)),
        "od_w_out": nrm(ks[16], (N_ODD, GLA_HEADS * GLA_DV, D_MODEL), GLA_HEADS * GLA_DV, resid),
        "mlp_w1": nrm(ks[17], (DEPTH, D_MODEL, D_FF), D_MODEL),
        "mlp_w2": nrm(ks[18], (DEPTH, D_FF, D_MODEL), D_FF, resid),
    }


def reference(x, positions, mix_norm_g, mlp_norm_g, final_norm_g, ev_w_in, ev_conv_w,
              ev_q_norm_g, ev_w_qb, ev_kv_norm_g, ev_w_kvb, ev_w_out, od_w_in, od_w_gate2,
              od_b_gate2, od_o_norm_g, od_w_out, mlp_w1, mlp_w2):
    inv_freq = 1.0 / (ROPE_THETA ** (jnp.arange(0, MLA_ROPE, 2, dtype=jnp.float32) / MLA_ROPE))
    ang = positions.astype(jnp.float32)[..., None] * inv_freq
    cos = jnp.cos(ang).astype(x.dtype)
    sin = jnp.sin(ang).astype(x.dtype)
    for layer in range(DEPTH):
        j = layer // 2
        h = rms_norm(x, mix_norm_g[layer])
        if layer % 2 == 0:
            x = x + even_mixer(h, cos, sin, ev_w_in[j], ev_conv_w[j], ev_q_norm_g[j], ev_w_qb[j],
                               ev_kv_norm_g[j], ev_w_kvb[j], ev_w_out[j])
        else:
            x = x + odd_mixer(h, od_w_in[j], od_w_gate2[j], od_b_gate2[j], od_o_norm_g[j], od_w_out[j])
        h = rms_norm(x, mlp_norm_g[layer])
        x = x + jnp.square(jax.nn.relu(h @ mlp_w1[layer])) @ mlp_w2[layer]
    return rms_norm(x, final_norm_g)
```

```python
import functools

import jax
import jax.numpy as jnp
from jax import lax
from jax.experimental import pallas as pl
from jax.experimental.pallas import tpu as pltpu

D_MODEL = 1024
CONV_W = 512
N_HEADS = 8
NOPE = 64
ROPE = 32
V_DIM = 64
Q_RANK = 384
KV_RANK = 256
ROPE_THETA = 10000.0
HEAD_PAD = 128
GLA_HEADS = 4
GLA_DK = 128
GLA_DV = 256
GATE_RANK = 16
GATE_PAD = 128
GATE_TAU = 16.0
GLA_CHUNK = 64
D_FF = 4 * D_MODEL
EPS = 1e-6

BF16 = jnp.bfloat16
F32 = jnp.float32

VMEM_LIMIT_BYTES = 56 * 1024 * 1024


def _rms(x, g):
    ms = jnp.mean(x * x, axis=-1, keepdims=True)
    return x * lax.rsqrt(ms + EPS) * g


def _dot(a, b):
    return jnp.dot(a, b, preferred_element_type=F32)


def _const_spec(shape):
    nd = len(shape)
    return pl.BlockSpec(shape, lambda *_: (0,) * nd, pipeline_mode=pl.Buffered(1))


def _params(*sem):
    return pltpu.CompilerParams(dimension_semantics=sem, vmem_limit_bytes=VMEM_LIMIT_BYTES)


def _rope(p, c, sa, sbn):
    return p * c + pltpu.roll(p, 16, 1) * sa + pltpu.roll(p, 112, 1) * sbn


def _even_in_kernel(x_ref, g_ref, wabc_ref, wqp_ref, wkv_ref, cw_ref, qg_ref, wqb_ref, kvg_ref,
                    wkb_ref, wvb_ref, c_ref, sa_ref, sbn_ref,
                    ya_ref, q_ref, k_ref, v_ref, carry_ref, *, tiles_per_seq, q_scale):
    tm = x_ref.shape[0]
    i = pl.program_id(0)
    h = _rms(x_ref[...], g_ref[...]).astype(BF16)

    a_c = _dot(h, wabc_ref[:, CONV_W:2 * CONV_W])
    a_v = _dot(h, wabc_ref[:, 2 * CONV_W:3 * CONV_W])
    u = a_c * a_v

    @pl.when(i % tiles_per_seq == 0)
    def _():
        carry_ref[...] = jnp.zeros_like(carry_ref)

    prev1 = carry_ref[7:8, :]
    prev2 = carry_ref[6:7, :]
    row = lax.broadcasted_iota(jnp.int32, u.shape, 0)
    u1 = jnp.where(row == 0, prev1, pltpu.roll(u, 1, 0))
    u2 = jnp.where(row == 0, prev2, jnp.where(row == 1, prev1, pltpu.roll(u, 2, 0)))
    carry_ref[...] = u[tm - 8:, :]
    conv = cw_ref[0:1, :] * u2 + cw_ref[1:2, :] * u1 + cw_ref[2:3, :] * u
    a_b = _dot(h, wabc_ref[:, 0:CONV_W])
    ya_ref[...] = (a_b * conv).astype(BF16)

    c = c_ref[...]
    sa = sa_ref[...]
    sbn = sbn_ref[...]
    zqp = _dot(h, wqp_ref[...])
    qn = _rms(zqp[:, :Q_RANK], qg_ref[...]).astype(BF16)
    k_pe = _rope(zqp[:, Q_RANK:], c, sa, sbn)
    qp = _dot(qn, wqb_ref[...])
    for hd in range(N_HEADS):
        sl = slice(hd * HEAD_PAD, (hd + 1) * HEAD_PAD)
        q_ref[:, sl] = (_rope(qp[:, sl], c, sa, sbn) * q_scale).astype(BF16)
    kvn = _rms(_dot(h, wkv_ref[...]), kvg_ref[...]).astype(BF16)
    kp = _dot(kvn, wkb_ref[...])
    for hd in range(N_HEADS):
        sl = slice(hd * HEAD_PAD, (hd + 1) * HEAD_PAD)
        k_ref[:, sl] = (kp[:, sl] + k_pe).astype(BF16)
    v_ref[...] = _dot(kvn, wvb_ref[...]).astype(BF16)


def _even_in(x, g, wabc, wqp, wkv, cw, qg, wqb, kvg, wkb, wvb, c, sa, sbn, *, seq, tm):
    t = x.shape[0]
    row = lambda w: pl.BlockSpec((tm, w), lambda i: (i, 0))
    q_scale = float((NOPE + ROPE) ** -0.5)
    return pl.pallas_call(
        functools.partial(_even_in_kernel, tiles_per_seq=seq // tm, q_scale=q_scale),
        grid=(t // tm,),
        in_specs=[row(D_MODEL), _const_spec(g.shape), _const_spec(wabc.shape), _const_spec(wqp.shape),
                  _const_spec(wkv.shape), _const_spec(cw.shape), _const_spec(qg.shape),
                  _const_spec(wqb.shape), _const_spec(kvg.shape), _const_spec(wkb.shape),
                  _const_spec(wvb.shape), row(HEAD_PAD), row(HEAD_PAD), row(HEAD_PAD)],
        out_specs=[row(CONV_W), row(N_HEADS * HEAD_PAD), row(N_HEADS * HEAD_PAD), row(N_HEADS * V_DIM)],
        out_shape=[jax.ShapeDtypeStruct((t, CONV_W), BF16),
                   jax.ShapeDtypeStruct((t, N_HEADS * HEAD_PAD), BF16),
                   jax.ShapeDtypeStruct((t, N_HEADS * HEAD_PAD), BF16),
                   jax.ShapeDtypeStruct((t, N_HEADS * V_DIM), BF16)],
        scratch_shapes=[pltpu.VMEM((8, CONV_W), F32)],
        compiler_params=_params("arbitrary"),
        name="even_in",
    )(x, g, wabc, wqp, wkv, cw, qg, wqb, kvg, wkb, wvb, c, sa, sbn)


def _attn_kernel(q_ref, k_ref, v_ref, o_ref, *, tk):
    tq = q_ref.shape[0]
    qi = pl.program_id(2)
    row = lax.broadcasted_iota(jnp.int32, (tq, tk), 0)
    col = lax.broadcasted_iota(jnp.int32, (tq, tk), 1)
    causal = col <= row
    outs = []
    for hd in range(2):
        q = q_ref[:, hd * HEAD_PAD:(hd + 1) * HEAD_PAD]

        def step(j, carry, masked, q=q, hd=hd):
            m, l, acc = carry
            start = pl.multiple_of(j * tk, tk)
            kj = k_ref[pl.ds(start, tk), hd * HEAD_PAD:(hd + 1) * HEAD_PAD]
            vj = v_ref[pl.ds(start, tk), :]
            s = lax.dot_general(q, kj, (((1,), (1,)), ((), ())), preferred_element_type=F32)
            if masked:
                s = jnp.where(causal, s, -jnp.inf)
            m_new = jnp.maximum(m, jnp.max(s, axis=-1, keepdims=True))
            alpha = jnp.exp(m - m_new)
            p = jnp.exp(s - m_new)
            l = alpha * l + jnp.sum(p, axis=-1, keepdims=True)
            acc = alpha * acc + _dot(p.astype(BF16), vj)
            return m_new, l, acc

        init = (jnp.full((tq, 1), -jnp.inf, F32), jnp.zeros((tq, 1), F32),
                jnp.zeros((tq, 2 * V_DIM), F32))
        carry = lax.fori_loop(0, qi, functools.partial(step, masked=False), init)
        _, l, acc = step(qi, carry, masked=True)
        outs.append(acc / l)
    lane = lax.broadcasted_iota(jnp.int32, (tq, 2 * V_DIM), 1)
    o_ref[...] = jnp.where(lane < V_DIM, outs[0], outs[1]).astype(BF16)


def _attention(q, k, v, *, batch, seq, tq):
    t = q.shape[0]
    nq = seq // tq
    return pl.pallas_call(
        functools.partial(_attn_kernel, tk=tq),
        grid=(batch, N_HEADS // 2, nq),
        in_specs=[pl.BlockSpec((tq, 2 * HEAD_PAD), lambda b, p, i: (b * nq + i, p)),
                  pl.BlockSpec((seq, 2 * HEAD_PAD), lambda b, p, i: (b, p)),
                  pl.BlockSpec((seq, 2 * V_DIM), lambda b, p, i: (b, p))],
        out_specs=pl.BlockSpec((tq, 2 * V_DIM), lambda b, p, i: (b * nq + i, p)),
        out_shape=jax.ShapeDtypeStruct((t, N_HEADS * V_DIM), BF16),
        compiler_params=_params("arbitrary", "arbitrary", "arbitrary"),
        name="attention",
    )(q, k, v)


def _chunk_cumsum(x):
    pos = lax.broadcasted_iota(jnp.int32, x.shape, 0) & (GLA_CHUNK - 1)
    d = 1
    while d < GLA_CHUNK:
        x = x + jnp.where(pos >= d, pltpu.roll(x, d, 0), 0.0)
        d *= 2
    return x


def _odd_in_kernel(x_ref, g_ref, wq_ref, wk_ref, wv_ref, wg_ref, wl_ref, wg2_ref, bg2_ref,
                   qe_ref, ke_ref, kl_ref, v_ref, gate_ref, dec_ref, *, q_scale):
    tm = x_ref.shape[0]
    nc = tm // GLA_CHUNK
    dqk = GLA_HEADS * GLA_DK
    h = _rms(x_ref[...], g_ref[...]).astype(BF16)
    g_low = _dot(h, wl_ref[...]).astype(BF16)
    gate_in = _dot(g_low, wg2_ref[...]) + bg2_ref[...]
    log_a = (jnp.minimum(gate_in, 0.0) - jnp.log1p(jnp.exp(-jnp.abs(gate_in)))) * (1.0 / GATE_TAU)
    bc = _chunk_cumsum(log_a)
    q = _dot(h, wq_ref[...])
    qe_ref[...] = (q * q_scale * jnp.exp(bc)).astype(BF16)
    k = _dot(h, wk_ref[...])
    ke_ref[...] = (k * jnp.exp(-bc)).astype(BF16)
    for ci in range(nc):
        rows = slice(ci * GLA_CHUNK, (ci + 1) * GLA_CHUNK)
        b_last = bc[(ci + 1) * GLA_CHUNK - 1:(ci + 1) * GLA_CHUNK, :]
        dec_ref[ci:ci + 1, :] = jnp.exp(b_last)
        kl_ref[rows, :] = (k[rows, :] * jnp.exp(b_last - bc[rows, :])).astype(BF16)
    v_ref[...] = _dot(h, wv_ref[...]).astype(BF16)
    gate_ref[...] = _dot(h, wg_ref[...])


def _odd_in(x, g, wq, wk, wv, wg, wl, wg2, bg2, *, tm):
    t = x.shape[0]
    dqk = GLA_HEADS * GLA_DK
    dv = GLA_HEADS * GLA_DV
    row = lambda w: pl.BlockSpec((tm, w), lambda i: (i, 0))
    return pl.pallas_call(
        functools.partial(_odd_in_kernel, q_scale=float(GLA_DK ** -0.5)),
        grid=(t // tm,),
        in_specs=[row(D_MODEL)] + [_const_spec(a.shape) for a in (g, wq, wk, wv, wg, wl, wg2, bg2)],
        out_specs=[row(dqk), row(dqk), row(dqk), row(dv), row(dv),
                   pl.BlockSpec((tm // GLA_CHUNK, dqk), lambda i: (i, 0))],
        out_shape=[jax.ShapeDtypeStruct((t, dqk), BF16), jax.ShapeDtypeStruct((t, dqk), BF16),
                   jax.ShapeDtypeStruct((t, dqk), BF16), jax.ShapeDtypeStruct((t, dv), BF16),
                   jax.ShapeDtypeStruct((t, dv), F32),
                   jax.ShapeDtypeStruct((t // GLA_CHUNK, dqk), F32)],
        compiler_params=_params("arbitrary"),
        name="odd_in",
    )(x, g, wq, wk, wv, wg, wl, wg2, bg2)


def _gla_kernel(qe_ref, ke_ref, kl_ref, v_ref, gate_ref, dec_ref, og_ref, o_ref, state_ref):
    seq = qe_ref.shape[0]
    n_chunks = seq // GLA_CHUNK
    c = GLA_CHUNK
    tril = (lax.broadcasted_iota(jnp.int32, (c, c), 1) <= lax.broadcasted_iota(jnp.int32, (c, c), 0))
    state_ref[...] = jnp.zeros_like(state_ref)
    og = og_ref[...]

    def chunk(n, _):
        rows = pl.ds(pl.multiple_of(n * c, c), c)
        qe = qe_ref[rows, :]
        ke = ke_ref[rows, :]
        kl = kl_ref[rows, :]
        vv = v_ref[rows, :]
        state_t = state_ref[...]
        attn = lax.dot_general(qe, ke, (((1,), (1,)), ((), ())), preferred_element_type=F32)
        attn = jnp.where(tril, attn, 0.0).astype(BF16)
        o = _dot(attn, vv) + lax.dot_general(qe, state_t.astype(BF16), (((1,), (1,)), ((), ())),
                                             preferred_element_type=F32)
        vk = lax.dot_general(vv, kl, (((0,), (0,)), ((), ())), preferred_element_type=F32)
        dec = dec_ref[pl.ds(n, 1), :]
        state_ref[...] = state_t * dec + vk
        on = _rms(o, og)
        gt = gate_ref[rows, :]
        o_ref[rows, :] = (on * (gt * jax.nn.sigmoid(gt))).astype(BF16)
        return 0

    lax.fori_loop(0, n_chunks, chunk, 0)


def _gla(qe, ke, kl, v, gate, dec_t, og, *, batch, seq):
    t = qe.shape[0]
    hb = lambda w: pl.BlockSpec((seq, w), lambda b, h: (b, h))
    return pl.pallas_call(
        _gla_kernel,
        grid=(batch, GLA_HEADS),
        in_specs=[hb(GLA_DK), hb(GLA_DK), hb(GLA_DK), hb(GLA_DV), hb(GLA_DV),
                  pl.BlockSpec((seq // GLA_CHUNK, GLA_DK), lambda b, h: (b, h)),
                  _const_spec(og.shape)],
        out_specs=hb(GLA_DV),
        out_shape=jax.ShapeDtypeStruct((t, GLA_HEADS * GLA_DV), BF16),
        scratch_shapes=[pltpu.VMEM((GLA_DV, GLA_DK), F32)],
        compiler_params=_params("arbitrary", "arbitrary"),
        name="gla",
    )(qe, ke, kl, v, gate, dec_t, og)


def _mix_mlp_kernel(x_ref, mix_ref, wo_ref, g_ref, w1_ref, w2_ref, fg_ref, o_ref, *, ff_chunk, final_norm):
    y = x_ref[...] + _dot(mix_ref[...], wo_ref[...])
    h = _rms(y, g_ref[...]).astype(BF16)
    acc = y
    for c0 in range(0, D_FF, ff_chunk):
        u = _dot(h, w1_ref[:, c0:c0 + ff_chunk])
        u = jnp.square(jnp.maximum(u, 0.0)).astype(BF16)
        acc = acc + _dot(u, w2_ref[c0:c0 + ff_chunk, :])
    if final_norm:
        acc = _rms(acc, fg_ref[...])
    o_ref[...] = acc


def _mix_mlp(x, mix, wo, g, w1, w2, fg, *, tm, final_norm):
    t = x.shape[0]
    row = lambda w: pl.BlockSpec((tm, w), lambda i: (i, 0))
    return pl.pallas_call(
        functools.partial(_mix_mlp_kernel, ff_chunk=512, final_norm=final_norm),
        grid=(t // tm,),
        in_specs=[row(D_MODEL), row(mix.shape[1])] + [_const_spec(a.shape) for a in (wo, g, w1, w2, fg)],
        out_specs=row(D_MODEL),
        out_shape=jax.ShapeDtypeStruct((t, D_MODEL), F32),
        compiler_params=_params("arbitrary"),
        name="mix_mlp",
    )(x, mix, wo, g, w1, w2, fg)


def _rope_tables(positions):
    inv_freq = 1.0 / (ROPE_THETA ** (jnp.arange(0, ROPE, 2, dtype=F32) / ROPE))
    ang = positions.astype(F32).reshape(-1, 1) * inv_freq
    cos = jnp.cos(ang)
    sin = jnp.sin(ang)
    t = ang.shape[0]
    half = ROPE // 2
    pad = HEAD_PAD - NOPE - ROPE
    c = jnp.concatenate([jnp.ones((t, NOPE), F32), cos, cos, jnp.zeros((t, pad), F32)], axis=1)
    sa = jnp.concatenate([jnp.zeros((t, NOPE + half), F32), sin, jnp.zeros((t, pad), F32)], axis=1)
    sbn = jnp.concatenate([jnp.zeros((t, NOPE), F32), -sin, jnp.zeros((t, half + pad), F32)], axis=1)
    return c, sa, sbn


def _even_weights(w_in, w_qb, w_kvb):
    a_end = 3 * CONV_W
    q_end = a_end + Q_RANK
    kv_end = q_end + KV_RANK
    zeros = lambda n: jnp.zeros((D_MODEL, n), w_in.dtype)
    wabc = w_in[:, :a_end]
    wqp = jnp.concatenate([w_in[:, a_end:q_end], zeros(NOPE), w_in[:, kv_end:], zeros(HEAD_PAD - NOPE - ROPE)],
                          axis=1)
    wkv = w_in[:, q_end:kv_end]
    wqb = jnp.pad(w_qb.reshape(Q_RANK, N_HEADS, NOPE + ROPE), ((0, 0), (0, 0), (0, HEAD_PAD - NOPE - ROPE)))
    wkvb = w_kvb.reshape(KV_RANK, N_HEADS, NOPE + V_DIM)
    wkb = jnp.pad(wkvb[:, :, :NOPE], ((0, 0), (0, 0), (0, HEAD_PAD - NOPE)))
    wvb = wkvb[:, :, NOPE:]
    cast = lambda a: a.astype(BF16)
    return (cast(wabc), cast(wqp), cast(wkv), cast(wqb.reshape(Q_RANK, N_HEADS * HEAD_PAD)),
            cast(wkb.reshape(KV_RANK, N_HEADS * HEAD_PAD)), cast(wvb.reshape(KV_RANK, N_HEADS * V_DIM)))


def _odd_weights(w_in, w_gate2):
    dqk = GLA_HEADS * GLA_DK
    dv = GLA_HEADS * GLA_DV
    cast = lambda a: a.astype(BF16)
    wq = w_in[:, :dqk]
    wk = w_in[:, dqk:2 * dqk]
    wv = w_in[:, 2 * dqk:2 * dqk + dv]
    wg = w_in[:, 2 * dqk + dv:2 * dqk + 2 * dv]
    wl = jnp.pad(w_in[:, 2 * dqk + 2 * dv:], ((0, 0), (0, GATE_PAD - GATE_RANK)))
    wg2 = jnp.pad(w_gate2, ((0, GATE_PAD - GATE_RANK), (0, 0)))
    return cast(wq), cast(wk), cast(wv), cast(wg), cast(wl), cast(wg2)


def kernel(x, positions, mix_norm_g, mlp_norm_g, final_norm_g, ev_w_in, ev_conv_w, ev_q_norm_g, ev_w_qb,
           ev_kv_norm_g, ev_w_kvb, ev_w_out, od_w_in, od_w_gate2, od_b_gate2, od_o_norm_g, od_w_out,
           mlp_w1, mlp_w2):
    batch, seq, _ = x.shape
    depth = mix_norm_g.shape[0]
    tm = min(512, seq)
    tq = min(512, seq)
    xf = x.reshape(batch * seq, D_MODEL)
    c, sa, sbn = _rope_tables(positions)
    fg = final_norm_g.reshape(1, D_MODEL)
    for layer in range(depth):
        j = layer // 2
        g_mix = mix_norm_g[layer].reshape(1, D_MODEL)
        if layer % 2 == 0:
            wabc, wqp, wkv, wqb, wkb, wvb = _even_weights(ev_w_in[j], ev_w_qb[j], ev_w_kvb[j])
            ya, q, k, v = _even_in(xf, g_mix, wabc, wqp, wkv, ev_conv_w[j], ev_q_norm_g[j].reshape(1, Q_RANK),
                                   wqb, ev_kv_norm_g[j].reshape(1, KV_RANK), wkb, wvb, c, sa, sbn,
                                   seq=seq, tm=tm)
            yb = _attention(q, k, v, batch=batch, seq=seq, tq=tq)
            mix = jnp.concatenate([ya, yb], axis=1)
            wo = ev_w_out[j].astype(BF16)
        else:
            wq, wk, wv, wg, wl, wg2 = _odd_weights(od_w_in[j], od_w_gate2[j])
            qe, ke, kl, v, gate, dec = _odd_in(xf, g_mix, wq, wk, wv, wg, wl, wg2,
                                               od_b_gate2[j].reshape(1, GLA_HEADS * GLA_DK), tm=tm)
            mix = _gla(qe, ke, kl, v, gate, dec, od_o_norm_g[j].reshape(1, GLA_DV), batch=batch, seq=seq)
            wo = od_w_out[j].astype(BF16)
        xf = _mix_mlp(xf, mix, wo, mlp_norm_g[layer].reshape(1, D_MODEL), mlp_w1[layer].astype(BF16),
                      mlp_w2[layer].astype(BF16), fg, tm=tm, final_norm=(layer == depth - 1))
    return xf.reshape(batch, seq, D_MODEL)
```

```python
import functools

import jax
import jax.numpy as jnp
import numpy as np
from jax import lax
from jax.experimental import pallas as pl
from jax.experimental.pallas import tpu as pltpu

D_MODEL = 1024
CONV_W = 512
N_HEADS = 8
NOPE = 64
ROPE = 32
V_DIM = 64
Q_RANK = 384
KV_RANK = 256
ROPE_THETA = 10000.0
HEAD_PAD = 128
GLA_HEADS = 4
GLA_DK = 128
GLA_DV = 256
GATE_RANK = 16
GATE_PAD = 128
GATE_TAU = 16.0
GLA_CHUNK = 64
DEC_LANES = 128
D_FF = 4 * D_MODEL
EPS = 1e-6

BF16 = jnp.bfloat16
F32 = jnp.float32

VMEM_LIMIT_BYTES = 56 * 1024 * 1024


def _rms(x, g):
    ms = jnp.mean(x * x, axis=-1, keepdims=True)
    return x * lax.rsqrt(ms + EPS) * g


def _dot(a, b):
    return jnp.dot(a, b, preferred_element_type=F32)


def _const_spec(shape):
    nd = len(shape)
    return pl.BlockSpec(shape, lambda *_: (0,) * nd, pipeline_mode=pl.Buffered(1))


def _params(*sem):
    return pltpu.CompilerParams(dimension_semantics=sem, vmem_limit_bytes=VMEM_LIMIT_BYTES)


def _rope(p, c, sa, sbn):
    return p * c + pltpu.roll(p, 16, 1) * sa + pltpu.roll(p, 112, 1) * sbn


def _even_in_kernel(x_ref, g_ref, wabc_ref, wqp_ref, wkv_ref, cw_ref, qg_ref, wqb_ref, kvg_ref,
                    wkb_ref, wvb_ref, vones_ref, c_ref, sa_ref, sbn_ref,
                    ya_ref, q_ref, k_ref, v_ref, carry_ref, *, tiles_per_seq, q_scale):
    tm = x_ref.shape[0]
    i = pl.program_id(0)
    h = _rms(x_ref[...], g_ref[...]).astype(BF16)

    a_c = _dot(h, wabc_ref[:, CONV_W:2 * CONV_W])
    a_v = _dot(h, wabc_ref[:, 2 * CONV_W:3 * CONV_W])
    u = a_c * a_v

    @pl.when(i % tiles_per_seq == 0)
    def _():
        carry_ref[...] = jnp.zeros_like(carry_ref)

    prev1 = carry_ref[7:8, :]
    prev2 = carry_ref[6:7, :]
    row = lax.broadcasted_iota(jnp.int32, u.shape, 0)
    u1 = jnp.where(row == 0, prev1, pltpu.roll(u, 1, 0))
    u2 = jnp.where(row == 0, prev2, jnp.where(row == 1, prev1, pltpu.roll(u, 2, 0)))
    carry_ref[...] = u[tm - 8:, :]
    conv = cw_ref[0:1, :] * u2 + cw_ref[1:2, :] * u1 + cw_ref[2:3, :] * u
    a_b = _dot(h, wabc_ref[:, 0:CONV_W])
    ya_ref[...] = (a_b * conv).astype(BF16)

    c = c_ref[...]
    sa = sa_ref[...]
    sbn = sbn_ref[...]
    zqp = _dot(h, wqp_ref[...])
    qn = _rms(zqp[:, :Q_RANK], qg_ref[...]).astype(BF16)
    k_pe = _rope(zqp[:, Q_RANK:], c, sa, sbn)
    qp = _dot(qn, wqb_ref[...])
    for hd in range(N_HEADS):
        sl = slice(hd * HEAD_PAD, (hd + 1) * HEAD_PAD)
        q_ref[:, sl] = (_rope(qp[:, sl], c, sa, sbn) * q_scale).astype(BF16)
    kvn = _rms(_dot(h, wkv_ref[...]), kvg_ref[...]).astype(BF16)
    kp = _dot(kvn, wkb_ref[...])
    for hd in range(N_HEADS):
        sl = slice(hd * HEAD_PAD, (hd + 1) * HEAD_PAD)
        k_ref[:, sl] = (kp[:, sl] + k_pe).astype(BF16)
    v_ref[...] = (_dot(kvn, wvb_ref[...]) + vones_ref[...]).astype(BF16)


def _even_in(x, g, wabc, wqp, wkv, cw, qg, wqb, kvg, wkb, wvb, vones, c, sa, sbn, *, seq, tm):
    t = x.shape[0]
    row = lambda w: pl.BlockSpec((tm, w), lambda i: (i, 0))
    q_scale = float((NOPE + ROPE) ** -0.5 * np.log2(np.e))
    return pl.pallas_call(
        functools.partial(_even_in_kernel, tiles_per_seq=seq // tm, q_scale=q_scale),
        grid=(t // tm,),
        in_specs=[row(D_MODEL), _const_spec(g.shape), _const_spec(wabc.shape), _const_spec(wqp.shape),
                  _const_spec(wkv.shape), _const_spec(cw.shape), _const_spec(qg.shape),
                  _const_spec(wqb.shape), _const_spec(kvg.shape), _const_spec(wkb.shape),
                  _const_spec(wvb.shape), _const_spec(vones.shape), row(HEAD_PAD), row(HEAD_PAD), row(HEAD_PAD)],
        out_specs=[row(CONV_W), row(N_HEADS * HEAD_PAD), row(N_HEADS * HEAD_PAD), row(N_HEADS * HEAD_PAD)],
        out_shape=[jax.ShapeDtypeStruct((t, CONV_W), BF16),
                   jax.ShapeDtypeStruct((t, N_HEADS * HEAD_PAD), BF16),
                   jax.ShapeDtypeStruct((t, N_HEADS * HEAD_PAD), BF16),
                   jax.ShapeDtypeStruct((t, N_HEADS * HEAD_PAD), BF16)],
        scratch_shapes=[pltpu.VMEM((8, CONV_W), F32)],
        compiler_params=_params("arbitrary"),
        name="even_in",
    )(x, g, wabc, wqp, wkv, cw, qg, wqb, kvg, wkb, wvb, vones, c, sa, sbn)


def _attn_kernel(q_ref, k_ref, v_ref, o_ref, *, tk):
    tq = q_ref.shape[0]
    n_sub = tq // tk
    qi = pl.program_id(2)
    chains = [(hd, a) for hd in range(2) for a in range(n_sub)]

    def update(carry, hd, a, start, width, diag):
        m, acc = carry
        lanes = slice(hd * HEAD_PAD, (hd + 1) * HEAD_PAD)
        q = q_ref[a * tk:(a + 1) * tk, lanes]
        s = lax.dot_general(q, k_ref[pl.ds(start, width), lanes], (((1,), (1,)), ((), ())),
                            preferred_element_type=F32)
        if diag:
            row = lax.broadcasted_iota(jnp.int32, (tk, width), 0)
            col = lax.broadcasted_iota(jnp.int32, (tk, width), 1)
            s = jnp.where(col <= row + (width - tk), s, -jnp.inf)
        m_new = jnp.maximum(m, jnp.max(s, axis=-1, keepdims=True))
        p = jnp.exp2(s - m_new).astype(BF16)
        acc = jnp.exp2(m - m_new) * acc + _dot(p, v_ref[pl.ds(start, width), lanes])
        return m_new, acc

    def full_step(j, carry):
        start = pl.multiple_of(j * tq, tq)
        return tuple(update(carry[c], hd, a, start, tq, False) for c, (hd, a) in enumerate(chains))

    init = tuple((jnp.full((tk, 1), -jnp.inf, F32), jnp.zeros((tk, HEAD_PAD), F32)) for _ in chains)
    carry = lax.fori_loop(0, qi, full_step, init)
    start = pl.multiple_of(qi * tq, tq)
    final = [update(carry[c], hd, a, start, (a + 1) * tk, True)[1] for c, (hd, a) in enumerate(chains)]
    lane = lax.broadcasted_iota(jnp.int32, (tk, HEAD_PAD), 1)
    for a in range(n_sub):
        acc0 = final[a]
        acc1 = final[n_sub + a]
        o0 = acc0 / acc0[:, V_DIM:V_DIM + 1]
        o1 = acc1 / acc1[:, 0:1]
        o_ref[a * tk:(a + 1) * tk, :] = jnp.where(lane < V_DIM, o0, o1).astype(BF16)


def _attention(q, k, v, *, batch, seq, tq, tk):
    t = q.shape[0]
    nq = seq // tq
    return pl.pallas_call(
        functools.partial(_attn_kernel, tk=tk),
        grid=(batch, N_HEADS // 2, nq),
        in_specs=[pl.BlockSpec((tq, 2 * HEAD_PAD), lambda b, p, i: (b * nq + i, p)),
                  pl.BlockSpec((seq, 2 * HEAD_PAD), lambda b, p, i: (b, p)),
                  pl.BlockSpec((seq, 2 * HEAD_PAD), lambda b, p, i: (b, p))],
        out_specs=pl.BlockSpec((tq, 2 * V_DIM), lambda b, p, i: (b * nq + i, p)),
        out_shape=jax.ShapeDtypeStruct((t, N_HEADS * V_DIM), BF16),
        compiler_params=_params("arbitrary", "arbitrary", "arbitrary"),
        name="attention",
    )(q, k, v)


def _chunk_cumsum(x):
    pos = lax.broadcasted_iota(jnp.int32, x.shape, 0) & (GLA_CHUNK - 1)
    d = 1
    while d < GLA_CHUNK:
        x = x + jnp.where(pos >= d, pltpu.roll(x, d, 0), 0.0)
        d *= 2
    return x


def _odd_in_kernel(x_ref, g_ref, wq_ref, wk_ref, wv_ref, wg_ref, wl_ref, wg2_ref, bg2_ref,
                   qe_ref, ke_ref, klt_lo_ref, klt_hi_ref, v_ref, gate_ref, dect_ref, *, q_scale):
    tm = x_ref.shape[0]
    nc = tm // GLA_CHUNK
    dqk = GLA_HEADS * GLA_DK
    h = _rms(x_ref[...], g_ref[...]).astype(BF16)
    g_low = _dot(h, wl_ref[...]).astype(BF16)
    gate_in = _dot(g_low, wg2_ref[...]) + bg2_ref[...]
    log_a = (jnp.minimum(gate_in, 0.0) - jnp.log1p(jnp.exp(-jnp.abs(gate_in)))) * (1.0 / GATE_TAU)
    bc = _chunk_cumsum(log_a)
    q = _dot(h, wq_ref[...])
    qe_ref[...] = (q * q_scale * jnp.exp(bc)).astype(BF16)
    k = _dot(h, wk_ref[...])
    ke_ref[...] = (k * jnp.exp(-bc)).astype(BF16)
    kl_parts = []
    dec_parts = []
    for ci in range(nc):
        rows = slice(ci * GLA_CHUNK, (ci + 1) * GLA_CHUNK)
        b_last = bc[(ci + 1) * GLA_CHUNK - 1:(ci + 1) * GLA_CHUNK, :]
        dec_parts.append(jnp.exp(b_last))
        kl_parts.append(k[rows, :] * jnp.exp(b_last - bc[rows, :]))
    klt = jnp.concatenate(kl_parts, axis=0).T
    odd_chunk = (lax.broadcasted_iota(jnp.int32, klt.shape, 1) & GLA_CHUNK) != 0
    klt_lo_ref[...] = jnp.where(odd_chunk, 0.0, klt).astype(BF16)
    klt_hi_ref[...] = jnp.where(odd_chunk, klt, 0.0).astype(BF16)
    dec_pad = jnp.concatenate(dec_parts + [jnp.zeros((DEC_LANES - nc, dqk), F32)], axis=0)
    dect_ref[...] = dec_pad.T
    v_ref[...] = _dot(h, wv_ref[...]).astype(BF16)
    gate_ref[...] = _dot(h, wg_ref[...])


def _odd_in(x, g, wq, wk, wv, wg, wl, wg2, bg2, *, tm):
    t = x.shape[0]
    dqk = GLA_HEADS * GLA_DK
    dv = GLA_HEADS * GLA_DV
    row = lambda w: pl.BlockSpec((tm, w), lambda i: (i, 0))
    col = pl.BlockSpec((dqk, tm), lambda i: (0, i))
    return pl.pallas_call(
        functools.partial(_odd_in_kernel, q_scale=float(GLA_DK ** -0.5)),
        grid=(t // tm,),
        in_specs=[row(D_MODEL)] + [_const_spec(a.shape) for a in (g, wq, wk, wv, wg, wl, wg2, bg2)],
        out_specs=[row(dqk), row(dqk), col, col, row(dv), row(dv),
                   pl.BlockSpec((dqk, DEC_LANES), lambda i: (i, 0))],
        out_shape=[jax.ShapeDtypeStruct((t, dqk), BF16), jax.ShapeDtypeStruct((t, dqk), BF16),
                   jax.ShapeDtypeStruct((dqk, t), BF16), jax.ShapeDtypeStruct((dqk, t), BF16),
                   jax.ShapeDtypeStruct((t, dv), BF16), jax.ShapeDtypeStruct((t, dv), F32),
                   jax.ShapeDtypeStruct((t // tm * dqk, DEC_LANES), F32)],
        compiler_params=_params("arbitrary"),
        name="odd_in",
    )(x, g, wq, wk, wv, wg, wl, wg2, bg2)


def _gla_kernel(qe_ref, ke_ref, lo0_ref, lo1_ref, hi0_ref, hi1_ref, v_ref, gate_ref, dect_ref, og_ref,
                o_ref, state_ref):
    nb, ts, _ = qe_ref.shape
    pair = 2 * GLA_CHUNK
    klt_refs = ((lo0_ref, hi0_ref), (lo1_ref, hi1_ref))

    @pl.when(pl.program_id(1) == 0)
    def _():
        state_ref[...] = jnp.zeros_like(state_ref)

    r = lax.broadcasted_iota(jnp.int32, (pair, pair), 0)
    c = lax.broadcasted_iota(jnp.int32, (pair, pair), 1)
    intra = (c <= r) & ((r < GLA_CHUNK) | (c >= GLA_CHUNK))
    og = og_ref[...]
    for cp in range(ts // pair):
        rows = slice(cp * pair, (cp + 1) * pair)
        for bi in range(nb):
            heads = []
            for hd in range(GLA_HEADS):
                dk = slice(hd * GLA_DK, (hd + 1) * GLA_DK)
                dv = slice(hd * GLA_DV, (hd + 1) * GLA_DV)
                qe = qe_ref[bi, rows, dk]
                vv = v_ref[bi, rows, dv]
                a = lax.dot_general(qe, ke_ref[bi, rows, dk], (((1,), (1,)), ((), ())),
                                    preferred_element_type=F32)
                a = jnp.where(intra, a, 0.0).astype(BF16)
                lhs = jnp.concatenate([a, klt_refs[bi][0][dk, rows], klt_refs[bi][1][dk, rows]], axis=0)
                res = _dot(lhs, vv)
                state = state_ref[bi, hd]
                outs = []
                for ci in range(2):
                    q_c = qe[ci * GLA_CHUNK:(ci + 1) * GLA_CHUNK]
                    outs.append(_dot(q_c, state.astype(BF16)))
                    dcol = dect_ref[bi, dk, 2 * cp + ci:2 * cp + ci + 1]
                    state = state * dcol + res[pair + ci * GLA_DK:pair + (ci + 1) * GLA_DK]
                state_ref[bi, hd] = state
                o = res[:pair] + jnp.concatenate(outs, axis=0)
                heads.append(_rms(o, og))
            gt = gate_ref[bi, rows, :]
            o_ref[bi, rows, :] = (jnp.concatenate(heads, axis=1) * (gt * jax.nn.sigmoid(gt))).astype(BF16)


def _gla(qe, ke, klt_lo, klt_hi, v, gate, dect, og, *, batch, seq, ts):
    nb = 2
    dqk = GLA_HEADS * GLA_DK
    dv = GLA_HEADS * GLA_DV
    nt = seq // ts
    r3 = lambda a: a.reshape(batch, seq, a.shape[-1])
    blk = lambda w: pl.BlockSpec((nb, ts, w), lambda p, t: (p, t, 0))
    klt = lambda bi: pl.BlockSpec((dqk, ts), lambda p, t: (0, (nb * p + bi) * nt + t))
    out = pl.pallas_call(
        _gla_kernel,
        grid=(batch // nb, nt),
        in_specs=[blk(dqk), blk(dqk), klt(0), klt(1), klt(0), klt(1), blk(dv), blk(dv),
                  pl.BlockSpec((nb, dqk, DEC_LANES), lambda p, t: (p, t, 0)), _const_spec(og.shape)],
        out_specs=blk(dv),
        out_shape=jax.ShapeDtypeStruct((batch, seq, dv), BF16),
        scratch_shapes=[pltpu.VMEM((nb, GLA_HEADS, GLA_DK, GLA_DV), F32)],
        compiler_params=_params("arbitrary", "arbitrary"),
        name="gla",
    )(r3(qe), r3(ke), klt_lo, klt_lo, klt_hi, klt_hi, r3(v), r3(gate),
      dect.reshape(batch, nt * dqk, DEC_LANES), og)
    return out.reshape(batch * seq, dv)


def _mix_mlp_kernel(x_ref, mix_ref, wo_ref, g_ref, w1_ref, w2_ref, fg_ref, o_ref, *, ff_chunk, final_norm):
    y = x_ref[...] + _dot(mix_ref[...], wo_ref[...])
    h = _rms(y, g_ref[...]).astype(BF16)
    acc = y
    for c0 in range(0, D_FF, ff_chunk):
        u = _dot(h, w1_ref[:, c0:c0 + ff_chunk])
        u = jnp.square(jnp.maximum(u, 0.0)).astype(BF16)
        acc = acc + _dot(u, w2_ref[c0:c0 + ff_chunk, :])
    if final_norm:
        acc = _rms(acc, fg_ref[...])
    o_ref[...] = acc


def _mix_mlp(x, mix, wo, g, w1, w2, fg, *, tm, final_norm):
    t = x.shape[0]
    row = lambda w: pl.BlockSpec((tm, w), lambda i: (i, 0))
    return pl.pallas_call(
        functools.partial(_mix_mlp_kernel, ff_chunk=512, final_norm=final_norm),
        grid=(t // tm,),
        in_specs=[row(D_MODEL), row(mix.shape[1])] + [_const_spec(a.shape) for a in (wo, g, w1, w2, fg)],
        out_specs=row(D_MODEL),
        out_shape=jax.ShapeDtypeStruct((t, D_MODEL), F32),
        compiler_params=_params("arbitrary"),
        name="mix_mlp",
    )(x, mix, wo, g, w1, w2, fg)


def _rope_tables(positions):
    inv_freq = 1.0 / (ROPE_THETA ** (jnp.arange(0, ROPE, 2, dtype=F32) / ROPE))
    ang = positions.astype(F32).reshape(-1, 1) * inv_freq
    cos = jnp.cos(ang)
    sin = jnp.sin(ang)
    t = ang.shape[0]
    half = ROPE // 2
    pad = HEAD_PAD - NOPE - ROPE
    c = jnp.concatenate([jnp.ones((t, NOPE), F32), cos, cos, jnp.zeros((t, pad), F32)], axis=1)
    sa = jnp.concatenate([jnp.zeros((t, NOPE + half), F32), sin, jnp.zeros((t, pad), F32)], axis=1)
    sbn = jnp.concatenate([jnp.zeros((t, NOPE), F32), -sin, jnp.zeros((t, half + pad), F32)], axis=1)
    return c, sa, sbn


def _even_weights(w_in, w_qb, w_kvb):
    a_end = 3 * CONV_W
    q_end = a_end + Q_RANK
    kv_end = q_end + KV_RANK
    zeros = lambda n: jnp.zeros((D_MODEL, n), w_in.dtype)
    wabc = w_in[:, :a_end]
    wqp = jnp.concatenate([w_in[:, a_end:q_end], zeros(NOPE), w_in[:, kv_end:], zeros(HEAD_PAD - NOPE - ROPE)],
                          axis=1)
    wkv = w_in[:, q_end:kv_end]
    wqb = jnp.pad(w_qb.reshape(Q_RANK, N_HEADS, NOPE + ROPE), ((0, 0), (0, 0), (0, HEAD_PAD - NOPE - ROPE)))
    wkvb = w_kvb.reshape(KV_RANK, N_HEADS, NOPE + V_DIM)
    wkb = jnp.pad(wkvb[:, :, :NOPE], ((0, 0), (0, 0), (0, HEAD_PAD - NOPE)))
    wv = wkvb[:, :, NOPE:]
    odd = (jnp.arange(N_HEADS) % 2 == 1)[None, :, None]
    wvb = jnp.where(odd, jnp.pad(wv, ((0, 0), (0, 0), (V_DIM, 0))), jnp.pad(wv, ((0, 0), (0, 0), (0, V_DIM))))
    cast = lambda a: a.astype(BF16)
    return (cast(wabc), cast(wqp), cast(wkv), cast(wqb.reshape(Q_RANK, N_HEADS * HEAD_PAD)),
            cast(wkb.reshape(KV_RANK, N_HEADS * HEAD_PAD)), cast(wvb.reshape(KV_RANK, N_HEADS * HEAD_PAD)))


def _v_ones_pattern():
    lane = np.arange(N_HEADS * HEAD_PAD) % (2 * HEAD_PAD)
    return jnp.asarray(((lane == V_DIM) | (lane == HEAD_PAD)).astype(np.float32)).reshape(1, -1)


def _odd_weights(w_in, w_gate2):
    dqk = GLA_HEADS * GLA_DK
    dv = GLA_HEADS * GLA_DV
    cast = lambda a: a.astype(BF16)
    wq = w_in[:, :dqk]
    wk = w_in[:, dqk:2 * dqk]
    wv = w_in[:, 2 * dqk:2 * dqk + dv]
    wg = w_in[:, 2 * dqk + dv:2 * dqk + 2 * dv]
    wl = jnp.pad(w_in[:, 2 * dqk + 2 * dv:], ((0, 0), (0, GATE_PAD - GATE_RANK)))
    wg2 = jnp.pad(w_gate2, ((0, GATE_PAD - GATE_RANK), (0, 0)))
    return cast(wq), cast(wk), cast(wv), cast(wg), cast(wl), cast(wg2)


def kernel(x, positions, mix_norm_g, mlp_norm_g, final_norm_g, ev_w_in, ev_conv_w, ev_q_norm_g, ev_w_qb,
           ev_kv_norm_g, ev_w_kvb, ev_w_out, od_w_in, od_w_gate2, od_b_gate2, od_o_norm_g, od_w_out,
           mlp_w1, mlp_w2):
    batch, seq, _ = x.shape
    depth = mix_norm_g.shape[0]
    tm = min(512, seq)
    tq = min(1024, seq)
    tk = min(512, tq)
    xf = x.reshape(batch * seq, D_MODEL)
    c, sa, sbn = _rope_tables(positions)
    fg = final_norm_g.reshape(1, D_MODEL)
    for layer in range(depth):
        j = layer // 2
        g_mix = mix_norm_g[layer].reshape(1, D_MODEL)
        if layer % 2 == 0:
            wabc, wqp, wkv, wqb, wkb, wvb = _even_weights(ev_w_in[j], ev_w_qb[j], ev_w_kvb[j])
            ya, q, k, v = _even_in(xf, g_mix, wabc, wqp, wkv, ev_conv_w[j], ev_q_norm_g[j].reshape(1, Q_RANK),
                                   wqb, ev_kv_norm_g[j].reshape(1, KV_RANK), wkb, wvb, _v_ones_pattern(),
                                   c, sa, sbn,
                                   seq=seq, tm=tm)
            yb = _attention(q, k, v, batch=batch, seq=seq, tq=tq, tk=tk)
            mix = jnp.concatenate([ya, yb], axis=1)
            wo = ev_w_out[j].astype(BF16)
        else:
            wq, wk, wv, wg, wl, wg2 = _odd_weights(od_w_in[j], od_w_gate2[j])
            qe, ke, klt_lo, klt_hi, v, gate, dect = _odd_in(xf, g_mix, wq, wk, wv, wg, wl, wg2,
                                                            od_b_gate2[j].reshape(1, GLA_HEADS * GLA_DK), tm=tm)
            mix = _gla(qe, ke, klt_lo, klt_hi, v, gate, dect, od_o_norm_g[j].reshape(1, GLA_DV),
                       batch=batch, seq=seq, ts=tm)
            wo = od_w_out[j].astype(BF16)
        xf = _mix_mlp(xf, mix, wo, mlp_norm_g[layer].reshape(1, D_MODEL), mlp_w1[layer].astype(BF16),
                      mlp_w2[layer].astype(BF16), fg, tm=tm, final_norm=(layer == depth - 1))
    return xf.reshape(batch, seq, D_MODEL)
```

```python
import functools

import jax
import jax.numpy as jnp
import numpy as np
from jax import lax
from jax.experimental import pallas as pl
from jax.experimental.pallas import tpu as pltpu

D_MODEL = 1024
CONV_W = 512
N_HEADS = 8
NOPE = 64
ROPE = 32
V_DIM = 64
Q_RANK = 384
KV_RANK = 256
ROPE_THETA = 10000.0
LANES = 128
HEAD_PAD = LANES
EVEN_IN_PAD = 3 * CONV_W + Q_RANK + KV_RANK + LANES
GLA_HEADS = 4
GLA_DK = 128
GLA_DV = 256
GLA_DQK = GLA_HEADS * GLA_DK
GLA_DVS = GLA_HEADS * GLA_DV
GATE_RANK = 16
GATE_PAD = LANES
ODD_IN_PAD = 2 * GLA_DQK + 2 * GLA_DVS + GATE_PAD
GATE_TAU = 16.0
GLA_CHUNK = 64
DEC_LANES = LANES
D_FF = 4 * D_MODEL
EPS = 1e-6

BF16 = jnp.bfloat16
F32 = jnp.float32

VMEM_LIMIT_BYTES = 56 * 1024 * 1024
ROW_TILE = 512
ATTN_Q_TILE = 1024
ATTN_SUB_TILE = 512
FF_CHUNK = 512


def _rms(x, g):
    ms = jnp.mean(x * x, axis=-1, keepdims=True)
    return x * lax.rsqrt(ms + EPS) * g


def _dot(a, b):
    return jnp.dot(a, b, preferred_element_type=F32)


def _dot_nt(a, b):
    return lax.dot_general(a, b, (((1,), (1,)), ((), ())), preferred_element_type=F32)


def _const_spec(shape):
    nd = len(shape)
    return pl.BlockSpec(shape, lambda *_: (0,) * nd, pipeline_mode=pl.Buffered(1))


def _layer_spec(arr, j):
    shape = arr.shape[1:]
    nd = len(shape)
    return pl.BlockSpec((None,) + shape, lambda *_: (j,) + (0,) * nd, pipeline_mode=pl.Buffered(1))


def _params(*sem):
    return pltpu.CompilerParams(dimension_semantics=sem, vmem_limit_bytes=VMEM_LIMIT_BYTES)


def _rope(p, c, sa, sbn):
    return p * c + pltpu.roll(p, 16, 1) * sa + pltpu.roll(p, 112, 1) * sbn


def _even_in_kernel(x_ref, g_ref, win_ref, cw_ref, qg_ref, wqb_ref, kvg_ref, wkb_ref, wvb_ref, vones_ref, cs_ref,
                    ya_ref, q_ref, k_ref, v_ref, carry_ref, *, tiles_per_seq, q_scale):
    tm = x_ref.shape[0]
    i = pl.program_id(0)
    h = _rms(x_ref[...], g_ref[...]).astype(BF16)

    a_c = _dot(h, win_ref[:, CONV_W:2 * CONV_W])
    a_v = _dot(h, win_ref[:, 2 * CONV_W:3 * CONV_W])
    u = a_c * a_v

    @pl.when(i % tiles_per_seq == 0)
    def _():
        carry_ref[...] = jnp.zeros_like(carry_ref)

    prev1 = carry_ref[7:8, :]
    prev2 = carry_ref[6:7, :]
    row = lax.broadcasted_iota(jnp.int32, u.shape, 0)
    u1 = jnp.where(row == 0, prev1, pltpu.roll(u, 1, 0))
    u2 = jnp.where(row == 0, prev2, jnp.where(row == 1, prev1, pltpu.roll(u, 2, 0)))
    carry_ref[...] = u[tm - 8:, :]
    conv = cw_ref[0:1, :] * u2 + cw_ref[1:2, :] * u1 + cw_ref[2:3, :] * u
    a_b = _dot(h, win_ref[:, 0:CONV_W])
    ya_ref[...] = (a_b * conv).astype(BF16)

    cs = cs_ref[...]
    lane = lax.broadcasted_iota(jnp.int32, cs.shape, 1)
    half = ROPE // 2
    c = jnp.where(lane < NOPE, 1.0, jnp.where(lane < NOPE + ROPE, pltpu.roll(cs, NOPE, 1), 0.0))
    sin = pltpu.roll(cs, NOPE - ROPE, 1)
    sa = jnp.where((lane >= NOPE + half) & (lane < NOPE + ROPE), sin, 0.0)
    sbn = jnp.where((lane >= NOPE) & (lane < NOPE + half), -sin, 0.0)

    zb = _dot(h, win_ref[:, 3 * CONV_W:])
    qn = _rms(zb[:, :Q_RANK], qg_ref[...]).astype(BF16)
    kvn = _rms(zb[:, Q_RANK:Q_RANK + KV_RANK], kvg_ref[...]).astype(BF16)
    k_pe = _rope(pltpu.roll(zb[:, Q_RANK + KV_RANK:], NOPE, 1), c, sa, sbn)
    qp = _dot(qn, wqb_ref[...])
    for hd in range(N_HEADS):
        sl = slice(hd * HEAD_PAD, (hd + 1) * HEAD_PAD)
        q_ref[:, sl] = (_rope(qp[:, sl], c, sa, sbn) * q_scale).astype(BF16)
    kp = _dot(kvn, wkb_ref[...])
    for hd in range(N_HEADS):
        sl = slice(hd * HEAD_PAD, (hd + 1) * HEAD_PAD)
        k_ref[:, sl] = (kp[:, sl] + k_pe).astype(BF16)
    v_ref[...] = (_dot(kvn, wvb_ref[...]) + vones_ref[...]).astype(BF16)


def _even_in(x, j, layer, w, cs, *, seq, tm):
    t = x.shape[0]
    row = lambda width: pl.BlockSpec((tm, width), lambda i: (i, 0))
    q_scale = float((NOPE + ROPE) ** -0.5 * np.log2(np.e))
    wide = jax.ShapeDtypeStruct((t, N_HEADS * HEAD_PAD), BF16)
    return pl.pallas_call(
        functools.partial(_even_in_kernel, tiles_per_seq=seq // tm, q_scale=q_scale),
        grid=(t // tm,),
        in_specs=[row(D_MODEL), _layer_spec(w["mix_g"], layer), _layer_spec(w["ev_in"], j),
                  _layer_spec(w["ev_conv"], j), _layer_spec(w["ev_qg"], j), _layer_spec(w["ev_qb"], j),
                  _layer_spec(w["ev_kvg"], j), _layer_spec(w["ev_kb"], j), _layer_spec(w["ev_vb"], j),
                  _const_spec(w["v_ones"].shape), row(LANES)],
        out_specs=[row(CONV_W), row(N_HEADS * HEAD_PAD), row(N_HEADS * HEAD_PAD), row(N_HEADS * HEAD_PAD)],
        out_shape=[jax.ShapeDtypeStruct((t, CONV_W), BF16), wide, wide, wide],
        scratch_shapes=[pltpu.VMEM((8, CONV_W), F32)],
        compiler_params=_params("arbitrary"),
        name="even_in",
    )(x, w["mix_g"], w["ev_in"], w["ev_conv"], w["ev_qg"], w["ev_qb"], w["ev_kvg"], w["ev_kb"], w["ev_vb"],
      w["v_ones"], cs)


def _attn_kernel(q_ref, k_ref, v_ref, o_ref, *, tk):
    tq = q_ref.shape[0]
    n_sub = tq // tk
    qi = pl.program_id(2)

    def update(m, acc, hd, r0, rows, start, width, diag):
        lanes = slice(hd * HEAD_PAD, (hd + 1) * HEAD_PAD)
        s = _dot_nt(q_ref[r0:r0 + rows, lanes], k_ref[pl.ds(start, width), lanes])
        if diag:
            row = lax.broadcasted_iota(jnp.int32, (rows, width), 0)
            col = lax.broadcasted_iota(jnp.int32, (rows, width), 1)
            s = jnp.where(col <= row + (width - rows), s, -jnp.inf)
        m_new = jnp.maximum(m, jnp.max(s, axis=-1, keepdims=True))
        p = jnp.exp2(s - m_new).astype(BF16)
        acc = jnp.exp2(m - m_new) * acc + _dot(p, v_ref[pl.ds(start, width), lanes])
        return m_new, acc

    def full_step(j, carry):
        start = pl.multiple_of(j * tq, tq)
        return tuple(update(*carry[hd], hd, 0, tq, start, tq, False) for hd in range(2))

    init = tuple((jnp.full((tq, 1), -jnp.inf, F32), jnp.zeros((tq, HEAD_PAD), F32)) for _ in range(2))
    carry = lax.fori_loop(0, qi, full_step, init)
    start = pl.multiple_of(qi * tq, tq)
    lane = lax.broadcasted_iota(jnp.int32, (tk, HEAD_PAD), 1)
    for a in range(n_sub):
        rows = slice(a * tk, (a + 1) * tk)
        accs = [update(carry[hd][0][rows], carry[hd][1][rows], hd, a * tk, tk, start, (a + 1) * tk, True)[1]
                for hd in range(2)]
        o0 = accs[0] / accs[0][:, V_DIM:V_DIM + 1]
        o1 = accs[1] / accs[1][:, 0:1]
        o_ref[rows, :] = jnp.where(lane < V_DIM, o0, o1).astype(BF16)


def _attention(q, k, v, *, batch, seq, tq, tk):
    t = q.shape[0]
    nq = seq // tq
    return pl.pallas_call(
        functools.partial(_attn_kernel, tk=tk),
        grid=(batch, N_HEADS // 2, nq),
        in_specs=[pl.BlockSpec((tq, 2 * HEAD_PAD), lambda b, p, i: (b * nq + i, p)),
                  pl.BlockSpec((seq, 2 * HEAD_PAD), lambda b, p, i: (b, p)),
                  pl.BlockSpec((seq, 2 * HEAD_PAD), lambda b, p, i: (b, p))],
        out_specs=pl.BlockSpec((tq, 2 * V_DIM), lambda b, p, i: (b * nq + i, p)),
        out_shape=jax.ShapeDtypeStruct((t, N_HEADS * V_DIM), BF16),
        compiler_params=_params("arbitrary", "arbitrary", "arbitrary"),
        name="attention",
    )(q, k, v)


def _chunk_cumsum(x):
    pos = lax.broadcasted_iota(jnp.int32, x.shape, 0) & (GLA_CHUNK - 1)
    d = 1
    while d < GLA_CHUNK:
        x = x + jnp.where(pos >= d, pltpu.roll(x, d, 0), 0.0)
        d *= 2
    return x


def _odd_in_kernel(x_ref, g_ref, win_ref, wg2_ref, bg2_ref,
                   qe_ref, ke_ref, klt_lo_ref, klt_hi_ref, v_ref, gate_ref, dect_ref, *, q_scale):
    tm = x_ref.shape[0]
    nc = tm // GLA_CHUNK
    q0, k0, v0, g0, l0 = 0, GLA_DQK, 2 * GLA_DQK, 2 * GLA_DQK + GLA_DVS, 2 * GLA_DQK + 2 * GLA_DVS
    h = _rms(x_ref[...], g_ref[...]).astype(BF16)
    g_low = _dot(h, win_ref[:, l0:]).astype(BF16)
    gate_in = _dot(g_low, wg2_ref[...]) + bg2_ref[...]
    log_a = (jnp.minimum(gate_in, 0.0) - jnp.log1p(jnp.exp(-jnp.abs(gate_in)))) * (1.0 / GATE_TAU)
    bc = _chunk_cumsum(log_a)
    q = _dot(h, win_ref[:, q0:k0])
    qe_ref[...] = (q * q_scale * jnp.exp(bc)).astype(BF16)
    k = _dot(h, win_ref[:, k0:v0])
    ke_ref[...] = (k * jnp.exp(-bc)).astype(BF16)
    kl_parts = []
    dec_parts = []
    for ci in range(nc):
        rows = slice(ci * GLA_CHUNK, (ci + 1) * GLA_CHUNK)
        b_last = bc[(ci + 1) * GLA_CHUNK - 1:(ci + 1) * GLA_CHUNK, :]
        dec_parts.append(jnp.exp(b_last))
        kl_parts.append(k[rows, :] * jnp.exp(b_last - bc[rows, :]))
    klt = jnp.concatenate(kl_parts, axis=0).T
    odd_chunk = (lax.broadcasted_iota(jnp.int32, klt.shape, 1) & GLA_CHUNK) != 0
    klt_lo_ref[...] = jnp.where(odd_chunk, 0.0, klt).astype(BF16)
    klt_hi_ref[...] = jnp.where(odd_chunk, klt, 0.0).astype(BF16)
    dec_pad = jnp.concatenate(dec_parts + [jnp.zeros((DEC_LANES - nc, GLA_DQK), F32)], axis=0)
    dect_ref[...] = dec_pad.T
    v_ref[...] = _dot(h, win_ref[:, v0:g0]).astype(BF16)
    gate_ref[...] = _dot(h, win_ref[:, g0:l0])


def _odd_in(x, j, layer, w, *, tm):
    t = x.shape[0]
    row = lambda width: pl.BlockSpec((tm, width), lambda i: (i, 0))
    col = pl.BlockSpec((GLA_DQK, tm), lambda i: (0, i))
    return pl.pallas_call(
        functools.partial(_odd_in_kernel, q_scale=float(GLA_DK ** -0.5)),
        grid=(t // tm,),
        in_specs=[row(D_MODEL), _layer_spec(w["mix_g"], layer), _layer_spec(w["od_in"], j),
                  _layer_spec(w["od_g2"], j), _layer_spec(w["od_b2"], j)],
        out_specs=[row(GLA_DQK), row(GLA_DQK), col, col, row(GLA_DVS), row(GLA_DVS),
                   pl.BlockSpec((GLA_DQK, DEC_LANES), lambda i: (i, 0))],
        out_shape=[jax.ShapeDtypeStruct((t, GLA_DQK), BF16), jax.ShapeDtypeStruct((t, GLA_DQK), BF16),
                   jax.ShapeDtypeStruct((GLA_DQK, t), BF16), jax.ShapeDtypeStruct((GLA_DQK, t), BF16),
                   jax.ShapeDtypeStruct((t, GLA_DVS), BF16), jax.ShapeDtypeStruct((t, GLA_DVS), F32),
                   jax.ShapeDtypeStruct((t // tm * GLA_DQK, DEC_LANES), F32)],
        compiler_params=_params("arbitrary"),
        name="odd_in",
    )(x, w["mix_g"], w["od_in"], w["od_g2"], w["od_b2"])


def _gla_kernel(qe_ref, ke_ref, lo0_ref, lo1_ref, hi0_ref, hi1_ref, v_ref, gate_ref, dect_ref, og_ref,
                o_ref, state_ref):
    nb, ts, _ = qe_ref.shape
    pair = 2 * GLA_CHUNK
    klt_refs = ((lo0_ref, hi0_ref), (lo1_ref, hi1_ref))

    @pl.when(pl.program_id(1) == 0)
    def _():
        state_ref[...] = jnp.zeros_like(state_ref)

    r = lax.broadcasted_iota(jnp.int32, (pair, pair), 0)
    c = lax.broadcasted_iota(jnp.int32, (pair, pair), 1)
    intra = (c <= r) & ((r < GLA_CHUNK) | (c >= GLA_CHUNK))
    og = og_ref[...]
    for cp in range(ts // pair):
        rows = slice(cp * pair, (cp + 1) * pair)
        for bi in range(nb):
            heads = []
            for hd in range(GLA_HEADS):
                dk = slice(hd * GLA_DK, (hd + 1) * GLA_DK)
                dv = slice(hd * GLA_DV, (hd + 1) * GLA_DV)
                qe = qe_ref[bi, rows, dk]
                vv = v_ref[bi, rows, dv]
                a = jnp.where(intra, _dot_nt(qe, ke_ref[bi, rows, dk]), 0.0).astype(BF16)
                lhs = jnp.concatenate([a, klt_refs[bi][0][dk, rows], klt_refs[bi][1][dk, rows]], axis=0)
                res = _dot(lhs, vv)
                state = state_ref[bi, hd]
                outs = []
                for ci in range(2):
                    q_c = qe[ci * GLA_CHUNK:(ci + 1) * GLA_CHUNK]
                    outs.append(_dot(q_c, state.astype(BF16)))
                    dcol = dect_ref[bi, dk, 2 * cp + ci:2 * cp + ci + 1]
                    state = state * dcol + res[pair + ci * GLA_DK:pair + (ci + 1) * GLA_DK]
                state_ref[bi, hd] = state
                o = res[:pair] + jnp.concatenate(outs, axis=0)
                heads.append(_rms(o, og))
            gt = gate_ref[bi, rows, :]
            o_ref[bi, rows, :] = (jnp.concatenate(heads, axis=1) * (gt * jax.nn.sigmoid(gt))).astype(BF16)


def _gla(qe, ke, klt_lo, klt_hi, v, gate, dect, j, w, *, batch, seq, ts):
    nb = 2
    nt = seq // ts
    r3 = lambda a: a.reshape(batch, seq, a.shape[-1])
    blk = lambda width: pl.BlockSpec((nb, ts, width), lambda p, t: (p, t, 0))
    klt = lambda bi: pl.BlockSpec((GLA_DQK, ts), lambda p, t: (0, (nb * p + bi) * nt + t))
    out = pl.pallas_call(
        _gla_kernel,
        grid=(batch // nb, nt),
        in_specs=[blk(GLA_DQK), blk(GLA_DQK), klt(0), klt(1), klt(0), klt(1), blk(GLA_DVS), blk(GLA_DVS),
                  pl.BlockSpec((nb, GLA_DQK, DEC_LANES), lambda p, t: (p, t, 0)), _layer_spec(w["od_og"], j)],
        out_specs=blk(GLA_DVS),
        out_shape=jax.ShapeDtypeStruct((batch, seq, GLA_DVS), BF16),
        scratch_shapes=[pltpu.VMEM((nb, GLA_HEADS, GLA_DK, GLA_DV), F32)],
        compiler_params=_params("arbitrary", "arbitrary"),
        name="gla",
    )(r3(qe), r3(ke), klt_lo, klt_lo, klt_hi, klt_hi, r3(v), r3(gate),
      dect.reshape(batch, nt * GLA_DQK, DEC_LANES), w["od_og"])
    return out.reshape(batch * seq, GLA_DVS)


def _mix_mlp_kernel(x_ref, ma_ref, mb_ref, wo_ref, g_ref, w1_ref, w2_ref, fg_ref, o_ref, *, final_norm):
    y = x_ref[...] + _dot(jnp.concatenate([ma_ref[...], mb_ref[...]], axis=1), wo_ref[...])
    h = _rms(y, g_ref[...]).astype(BF16)
    acc = y
    for c0 in range(0, D_FF, FF_CHUNK):
        u = _dot(h, w1_ref[:, c0:c0 + FF_CHUNK])
        u = jnp.square(jnp.maximum(u, 0.0)).astype(BF16)
        acc = acc + _dot(u, w2_ref[c0:c0 + FF_CHUNK, :])
    if final_norm:
        acc = _rms(acc, fg_ref[...])
    o_ref[...] = acc


def _mix_mlp(x, mix_a, mix_b, wo, j, layer, w, *, tm, final_norm):
    t = x.shape[0]
    half = D_MODEL // 2
    row = lambda width: pl.BlockSpec((tm, width), lambda i: (i, 0))
    b_spec = row(half) if mix_b is not mix_a else pl.BlockSpec((tm, half), lambda i: (i, 1))
    return pl.pallas_call(
        functools.partial(_mix_mlp_kernel, final_norm=final_norm),
        grid=(t // tm,),
        in_specs=[row(D_MODEL), row(half), b_spec, _layer_spec(wo, j), _layer_spec(w["mlp_g"], layer),
                  _layer_spec(w["mlp_w1"], layer), _layer_spec(w["mlp_w2"], layer),
                  _const_spec(w["final_g"].shape)],
        out_specs=row(D_MODEL),
        out_shape=jax.ShapeDtypeStruct((t, D_MODEL), F32),
        compiler_params=_params("arbitrary"),
        name="mix_mlp",
    )(x, mix_a, mix_b, wo, w["mlp_g"], w["mlp_w1"], w["mlp_w2"], w["final_g"])


def _rope_table(positions):
    inv_freq = 1.0 / (ROPE_THETA ** (jnp.arange(0, ROPE, 2, dtype=F32) / ROPE))
    ang = positions.astype(F32).reshape(-1, 1) * inv_freq
    cos = jnp.cos(ang)
    sin = jnp.sin(ang)
    return jnp.concatenate([cos, cos, sin, sin, jnp.zeros((ang.shape[0], LANES - 2 * ROPE), F32)], axis=1)


def _prepare_weights(mix_norm_g, mlp_norm_g, final_norm_g, ev_w_in, ev_conv_w, ev_q_norm_g, ev_w_qb,
                     ev_kv_norm_g, ev_w_kvb, ev_w_out, od_w_in, od_w_gate2, od_b_gate2, od_o_norm_g, od_w_out,
                     mlp_w1, mlp_w2):
    cast = lambda a: a.astype(BF16)
    vec = lambda a: a.reshape(a.shape[0], 1, a.shape[1])
    n_even = ev_w_in.shape[0]
    pad_last = lambda a, n: jnp.pad(a, [(0, 0)] * (a.ndim - 1) + [(0, n)])
    head_pad = HEAD_PAD - NOPE - ROPE
    wqb = pad_last(ev_w_qb.reshape(n_even, Q_RANK, N_HEADS, NOPE + ROPE), head_pad)
    wkvb = ev_w_kvb.reshape(n_even, KV_RANK, N_HEADS, NOPE + V_DIM)
    wkb = pad_last(wkvb[..., :NOPE], HEAD_PAD - NOPE)
    wv = wkvb[..., NOPE:]
    odd = (jnp.arange(N_HEADS) % 2 == 1)[None, None, :, None]
    wvb = jnp.where(odd, jnp.pad(wv, ((0, 0), (0, 0), (0, 0), (V_DIM, 0))), pad_last(wv, V_DIM))
    flat = lambda a: a.reshape(a.shape[0], a.shape[1], N_HEADS * HEAD_PAD)
    lane = np.arange(N_HEADS * HEAD_PAD) % (2 * HEAD_PAD)
    v_ones = jnp.asarray(((lane == V_DIM) | (lane == HEAD_PAD)).astype(np.float32)).reshape(1, -1)
    return {
        "mix_g": vec(mix_norm_g), "mlp_g": vec(mlp_norm_g), "final_g": final_norm_g.reshape(1, D_MODEL),
        "ev_in": cast(pad_last(ev_w_in, EVEN_IN_PAD - ev_w_in.shape[-1])), "ev_conv": ev_conv_w,
        "ev_qg": vec(ev_q_norm_g), "ev_qb": cast(flat(wqb)), "ev_kvg": vec(ev_kv_norm_g),
        "ev_kb": cast(flat(wkb)), "ev_vb": cast(flat(wvb)), "v_ones": v_ones, "ev_out": cast(ev_w_out),
        "od_in": cast(pad_last(od_w_in, ODD_IN_PAD - od_w_in.shape[-1])),
        "od_g2": cast(jnp.pad(od_w_gate2, ((0, 0), (0, GATE_PAD - GATE_RANK), (0, 0)))),
        "od_b2": vec(od_b_gate2), "od_og": vec(od_o_norm_g), "od_out": cast(od_w_out),
        "mlp_w1": cast(mlp_w1), "mlp_w2": cast(mlp_w2),
    }


def kernel(x, positions, mix_norm_g, mlp_norm_g, final_norm_g, ev_w_in, ev_conv_w, ev_q_norm_g, ev_w_qb,
           ev_kv_norm_g, ev_w_kvb, ev_w_out, od_w_in, od_w_gate2, od_b_gate2, od_o_norm_g, od_w_out,
           mlp_w1, mlp_w2):
    batch, seq, _ = x.shape
    depth = mix_norm_g.shape[0]
    tm = min(ROW_TILE, seq)
    tq = min(ATTN_Q_TILE, seq)
    tk = min(ATTN_SUB_TILE, tq)
    w = _prepare_weights(mix_norm_g, mlp_norm_g, final_norm_g, ev_w_in, ev_conv_w, ev_q_norm_g, ev_w_qb,
                         ev_kv_norm_g, ev_w_kvb, ev_w_out, od_w_in, od_w_gate2, od_b_gate2, od_o_norm_g,
                         od_w_out, mlp_w1, mlp_w2)
    cs = _rope_table(positions)
    xf = x.reshape(batch * seq, D_MODEL)
    for layer in range(depth):
        j = layer // 2
        if layer % 2 == 0:
            ya, q, k, v = _even_in(xf, j, layer, w, cs, seq=seq, tm=tm)
            yb = _attention(q, k, v, batch=batch, seq=seq, tq=tq, tk=tk)
            mix_a, mix_b, wo = ya, yb, w["ev_out"]
        else:
            qe, ke, klt_lo, klt_hi, v, gate, dect = _odd_in(xf, j, layer, w, tm=tm)
            mix_a = mix_b = _gla(qe, ke, klt_lo, klt_hi, v, gate, dect, j, w, batch=batch, seq=seq, ts=tm)
            wo = w["od_out"]
        xf = _mix_mlp(xf, mix_a, mix_b, wo, j, layer, w, tm=tm, final_norm=(layer == depth - 1))
    return xf.reshape(batch, seq, D_MODEL)
```

```python
import functools

import jax
import jax.numpy as jnp
import numpy as np
from jax import lax
from jax.experimental import pallas as pl
from jax.experimental.pallas import tpu as pltpu

D_MODEL = 1024
CONV_W = 512
N_HEADS = 8
NOPE = 64
ROPE = 32
V_DIM = 64
Q_RANK = 384
KV_RANK = 256
ROPE_THETA = 10000.0
LANES = 128
HEAD_PAD = LANES
EVEN_IN_PAD = 3 * CONV_W + Q_RANK + KV_RANK + LANES
GLA_HEADS = 4
GLA_DK = 128
GLA_DV = 256
GLA_DQK = GLA_HEADS * GLA_DK
GLA_DVS = GLA_HEADS * GLA_DV
GATE_RANK = 16
GATE_PAD = LANES
ODD_IN_PAD = 2 * GLA_DQK + 2 * GLA_DVS + GATE_PAD
GATE_TAU = 16.0
GLA_CHUNK = 64
DEC_LANES = LANES
D_FF = 4 * D_MODEL
EPS = 1e-6

BF16 = jnp.bfloat16
F32 = jnp.float32

VMEM_LIMIT_BYTES = 56 * 1024 * 1024
ROW_TILE = 512
ATTN_Q_TILE = 2048
ATTN_KV_STEP = 1024
ATTN_SUB_TILE = 512
FF_CHUNK = 512


def _rms(x, g):
    ms = jnp.mean(x * x, axis=-1, keepdims=True)
    return x * lax.rsqrt(ms + EPS) * g


def _dot(a, b):
    return jnp.dot(a, b, preferred_element_type=F32)


def _dot_nt(a, b):
    return lax.dot_general(a, b, (((1,), (1,)), ((), ())), preferred_element_type=F32)


def _const_spec(shape):
    nd = len(shape)
    return pl.BlockSpec(shape, lambda *_: (0,) * nd, pipeline_mode=pl.Buffered(1))


def _layer_spec(arr, j):
    shape = arr.shape[1:]
    nd = len(shape)
    return pl.BlockSpec((None,) + shape, lambda *_: (j,) + (0,) * nd, pipeline_mode=pl.Buffered(1))


def _params(*sem):
    return pltpu.CompilerParams(dimension_semantics=sem, vmem_limit_bytes=VMEM_LIMIT_BYTES)


def _rope(p, c, sa, sbn):
    return p * c + pltpu.roll(p, 16, 1) * sa + pltpu.roll(p, 112, 1) * sbn


def _even_in_kernel(x_ref, g_ref, win_ref, cw_ref, qg_ref, wqb_ref, kvg_ref, wkb_ref, wvb_ref, vones_ref, cs_ref,
                    ya_ref, q_ref, k_ref, v_ref, carry_ref, *, tiles_per_seq, q_scale):
    tm = x_ref.shape[0]
    i = pl.program_id(0)
    h = _rms(x_ref[...], g_ref[...]).astype(BF16)

    a_c = _dot(h, win_ref[:, CONV_W:2 * CONV_W])
    a_v = _dot(h, win_ref[:, 2 * CONV_W:3 * CONV_W])
    u = a_c * a_v

    @pl.when(i % tiles_per_seq == 0)
    def _():
        carry_ref[...] = jnp.zeros_like(carry_ref)

    prev1 = carry_ref[7:8, :]
    prev2 = carry_ref[6:7, :]
    row = lax.broadcasted_iota(jnp.int32, u.shape, 0)
    u1 = jnp.where(row == 0, prev1, pltpu.roll(u, 1, 0))
    u2 = jnp.where(row == 0, prev2, jnp.where(row == 1, prev1, pltpu.roll(u, 2, 0)))
    carry_ref[...] = u[tm - 8:, :]
    conv = cw_ref[0:1, :] * u2 + cw_ref[1:2, :] * u1 + cw_ref[2:3, :] * u
    a_b = _dot(h, win_ref[:, 0:CONV_W])
    ya_ref[...] = (a_b * conv).astype(BF16)

    cst = cs_ref[...]
    cs = jnp.concatenate([cst, jnp.zeros((LANES - cst.shape[0], tm), F32)], axis=0).T
    lane = lax.broadcasted_iota(jnp.int32, cs.shape, 1)
    half = ROPE // 2
    c = jnp.where(lane < NOPE, 1.0, jnp.where(lane < NOPE + ROPE, pltpu.roll(cs, NOPE, 1), 0.0))
    sin = pltpu.roll(cs, NOPE - ROPE, 1)
    sa = jnp.where((lane >= NOPE + half) & (lane < NOPE + ROPE), sin, 0.0)
    sbn = jnp.where((lane >= NOPE) & (lane < NOPE + half), -sin, 0.0)

    zb = _dot(h, win_ref[:, 3 * CONV_W:])
    qn = _rms(zb[:, :Q_RANK], qg_ref[...]).astype(BF16)
    kvn = _rms(zb[:, Q_RANK:Q_RANK + KV_RANK], kvg_ref[...]).astype(BF16)
    k_pe = _rope(pltpu.roll(zb[:, Q_RANK + KV_RANK:], NOPE, 1), c, sa, sbn)
    qp = _dot(qn, wqb_ref[...])
    for hd in range(N_HEADS):
        sl = slice(hd * HEAD_PAD, (hd + 1) * HEAD_PAD)
        q_ref[:, sl] = (_rope(qp[:, sl], c, sa, sbn) * q_scale).astype(BF16)
    kp = _dot(kvn, wkb_ref[...])
    for hd in range(N_HEADS):
        sl = slice(hd * HEAD_PAD, (hd + 1) * HEAD_PAD)
        k_ref[:, sl] = (kp[:, sl] + k_pe).astype(BF16)
    v_ref[...] = (_dot(kvn, wvb_ref[...]) + vones_ref[...]).astype(BF16)


def _even_in(x, j, layer, w, cs, *, seq, tm):
    t = x.shape[0]
    row = lambda width: pl.BlockSpec((tm, width), lambda i: (i, 0))
    q_scale = float((NOPE + ROPE) ** -0.5 * np.log2(np.e))
    wide = jax.ShapeDtypeStruct((t, N_HEADS * HEAD_PAD), BF16)
    return pl.pallas_call(
        functools.partial(_even_in_kernel, tiles_per_seq=seq // tm, q_scale=q_scale),
        grid=(t // tm,),
        in_specs=[row(D_MODEL), _layer_spec(w["mix_g"], layer), _layer_spec(w["ev_in"], j),
                  _layer_spec(w["ev_conv"], j), _layer_spec(w["ev_qg"], j), _layer_spec(w["ev_qb"], j),
                  _layer_spec(w["ev_kvg"], j), _layer_spec(w["ev_kb"], j), _layer_spec(w["ev_vb"], j),
                  _const_spec(w["v_ones"].shape), pl.BlockSpec((2 * ROPE, tm), lambda i: (0, i))],
        out_specs=[row(CONV_W), row(N_HEADS * HEAD_PAD), row(N_HEADS * HEAD_PAD), row(N_HEADS * HEAD_PAD)],
        out_shape=[jax.ShapeDtypeStruct((t, CONV_W), BF16), wide, wide, wide],
        scratch_shapes=[pltpu.VMEM((8, CONV_W), F32)],
        compiler_params=_params("arbitrary"),
        name="even_in",
    )(x, w["mix_g"], w["ev_in"], w["ev_conv"], w["ev_qg"], w["ev_qb"], w["ev_kvg"], w["ev_kb"], w["ev_vb"],
      w["v_ones"], cs)


def _attn_kernel(q_ref, k_ref, v_ref, o_ref, *, tk, kv_step, n_q):
    tq = q_ref.shape[0]
    n_sub = tq // tk
    qi = pl.program_id(2)

    def update(state, hd, r0, rows, start, width, diag):
        lanes = slice(hd * HEAD_PAD, (hd + 1) * HEAD_PAD)
        s = _dot_nt(q_ref[r0:r0 + rows, lanes], k_ref[start:start + width, lanes])
        if diag:
            row = lax.broadcasted_iota(jnp.int32, (rows, width), 0)
            col = lax.broadcasted_iota(jnp.int32, (rows, width), 1)
            s = jnp.where(col <= row + (width - rows), s, -jnp.inf)
        m_blk = jnp.max(s, axis=-1, keepdims=True)
        if state is None:
            m_new = m_blk
        else:
            m, acc = state
            m_new = jnp.maximum(m, m_blk)
        pv = _dot(jnp.exp2(s - m_new).astype(BF16), v_ref[start:start + width, lanes])
        if state is not None:
            pv = jnp.exp2(m - m_new) * acc + pv
        return m_new, pv

    def tile_body(c):
        states = [None, None]
        for start in range(0, c * tq, kv_step):
            states = [update(states[hd], hd, 0, tq, start, kv_step, False) for hd in range(2)]
        lane = lax.broadcasted_iota(jnp.int32, (tk, HEAD_PAD), 1)
        for a in range(n_sub):
            rows = slice(a * tk, (a + 1) * tk)
            accs = []
            for hd in range(2):
                st = None if states[hd] is None else (states[hd][0][rows], states[hd][1][rows])
                accs.append(update(st, hd, a * tk, tk, c * tq, (a + 1) * tk, True)[1])
            o0 = accs[0] / accs[0][:, V_DIM:V_DIM + 1]
            o1 = accs[1] / accs[1][:, 0:1]
            o_ref[rows, :] = jnp.where(lane < V_DIM, o0, o1).astype(BF16)

    for c in range(n_q):
        pl.when(qi == c)(functools.partial(tile_body, c))


def _attention(q, k, v, *, batch, seq, tq, tk):
    t = q.shape[0]
    nq = seq // tq
    return pl.pallas_call(
        functools.partial(_attn_kernel, tk=tk, kv_step=min(ATTN_KV_STEP, tq), n_q=nq),
        grid=(batch, N_HEADS // 2, nq),
        in_specs=[pl.BlockSpec((tq, 2 * HEAD_PAD), lambda b, p, i: (b * nq + i, p)),
                  pl.BlockSpec((seq, 2 * HEAD_PAD), lambda b, p, i: (b, p)),
                  pl.BlockSpec((seq, 2 * HEAD_PAD), lambda b, p, i: (b, p))],
        out_specs=pl.BlockSpec((tq, 2 * V_DIM), lambda b, p, i: (b * nq + i, p)),
        out_shape=jax.ShapeDtypeStruct((t, N_HEADS * V_DIM), BF16),
        compiler_params=_params("arbitrary", "arbitrary", "arbitrary"),
        name="attention",
    )(q, k, v)


def _chunk_cumsum(x):
    pos = lax.broadcasted_iota(jnp.int32, x.shape, 0) & (GLA_CHUNK - 1)
    d = 1
    while d < GLA_CHUNK:
        x = x + jnp.where(pos >= d, pltpu.roll(x, d, 0), 0.0)
        d *= 2
    return x


def _odd_in_kernel(x_ref, g_ref, win_ref, wg2_ref, bg2_ref,
                   qe_ref, ke_ref, klt_lo_ref, klt_hi_ref, v_ref, gate_ref, dect_ref, *, q_scale):
    tm = x_ref.shape[0]
    nc = tm // GLA_CHUNK
    q0, k0, v0, g0, l0 = 0, GLA_DQK, 2 * GLA_DQK, 2 * GLA_DQK + GLA_DVS, 2 * GLA_DQK + 2 * GLA_DVS
    h = _rms(x_ref[...], g_ref[...]).astype(BF16)
    g_low = _dot(h, win_ref[:, l0:]).astype(BF16)
    gate_in = _dot(g_low, wg2_ref[...]) + bg2_ref[...]
    log_a = (jnp.minimum(gate_in, 0.0) - jnp.log1p(jnp.exp(-jnp.abs(gate_in)))) * (1.0 / GATE_TAU)
    bc = _chunk_cumsum(log_a)
    q = _dot(h, win_ref[:, q0:k0])
    qe_ref[...] = (q * q_scale * jnp.exp(bc)).astype(BF16)
    k = _dot(h, win_ref[:, k0:v0])
    ke_ref[...] = (k * jnp.exp(-bc)).astype(BF16)
    kl_parts = []
    dec_parts = []
    for ci in range(nc):
        rows = slice(ci * GLA_CHUNK, (ci + 1) * GLA_CHUNK)
        b_last = bc[(ci + 1) * GLA_CHUNK - 1:(ci + 1) * GLA_CHUNK, :]
        dec_parts.append(jnp.exp(b_last))
        kl_parts.append(k[rows, :] * jnp.exp(b_last - bc[rows, :]))
    klt = jnp.concatenate(kl_parts, axis=0).T
    odd_chunk = (lax.broadcasted_iota(jnp.int32, klt.shape, 1) & GLA_CHUNK) != 0
    klt_lo_ref[...] = jnp.where(odd_chunk, 0.0, klt).astype(BF16)
    klt_hi_ref[...] = jnp.where(odd_chunk, klt, 0.0).astype(BF16)
    dec_pad = jnp.concatenate(dec_parts + [jnp.zeros((DEC_LANES - nc, GLA_DQK), F32)], axis=0)
    dect_ref[...] = dec_pad.T
    v_ref[...] = _dot(h, win_ref[:, v0:g0]).astype(BF16)
    gate_ref[...] = _dot(h, win_ref[:, g0:l0])


def _odd_in(x, j, layer, w, *, tm):
    t = x.shape[0]
    row = lambda width: pl.BlockSpec((tm, width), lambda i: (i, 0))
    col = pl.BlockSpec((GLA_DQK, tm), lambda i: (0, i))
    return pl.pallas_call(
        functools.partial(_odd_in_kernel, q_scale=float(GLA_DK ** -0.5)),
        grid=(t // tm,),
        in_specs=[row(D_MODEL), _layer_spec(w["mix_g"], layer), _layer_spec(w["od_in"], j),
                  _layer_spec(w["od_g2"], j), _layer_spec(w["od_b2"], j)],
        out_specs=[row(GLA_DQK), row(GLA_DQK), col, col, row(GLA_DVS), row(GLA_DVS),
                   pl.BlockSpec((GLA_DQK, DEC_LANES), lambda i: (i, 0))],
        out_shape=[jax.ShapeDtypeStruct((t, GLA_DQK), BF16), jax.ShapeDtypeStruct((t, GLA_DQK), BF16),
                   jax.ShapeDtypeStruct((GLA_DQK, t), BF16), jax.ShapeDtypeStruct((GLA_DQK, t), BF16),
                   jax.ShapeDtypeStruct((t, GLA_DVS), BF16), jax.ShapeDtypeStruct((t, GLA_DVS), F32),
                   jax.ShapeDtypeStruct((t // tm * GLA_DQK, DEC_LANES), F32)],
        compiler_params=_params("arbitrary"),
        name="odd_in",
    )(x, w["mix_g"], w["od_in"], w["od_g2"], w["od_b2"])


def _gla_kernel(qe_ref, ke_ref, lo0_ref, lo1_ref, hi0_ref, hi1_ref, v_ref, gate_ref, dect_ref, og_ref,
                o_ref, state_ref):
    nb, ts, _ = qe_ref.shape
    pair = 2 * GLA_CHUNK
    klt_refs = ((lo0_ref, hi0_ref), (lo1_ref, hi1_ref))

    @pl.when(pl.program_id(1) == 0)
    def _():
        state_ref[...] = jnp.zeros_like(state_ref)

    r = lax.broadcasted_iota(jnp.int32, (pair, pair), 0)
    c = lax.broadcasted_iota(jnp.int32, (pair, pair), 1)
    intra = (c <= r) & ((r < GLA_CHUNK) | (c >= GLA_CHUNK))
    og = og_ref[...]
    for cp in range(ts // pair):
        rows = slice(cp * pair, (cp + 1) * pair)
        for bi in range(nb):
            heads = []
            for hd in range(GLA_HEADS):
                dk = slice(hd * GLA_DK, (hd + 1) * GLA_DK)
                dv = slice(hd * GLA_DV, (hd + 1) * GLA_DV)
                qe = qe_ref[bi, rows, dk]
                vv = v_ref[bi, rows, dv]
                a = jnp.where(intra, _dot_nt(qe, ke_ref[bi, rows, dk]), 0.0).astype(BF16)
                lhs = jnp.concatenate([a, klt_refs[bi][0][dk, rows], klt_refs[bi][1][dk, rows]], axis=0)
                res = _dot(lhs, vv)
                state = state_ref[bi, hd]
                outs = []
                for ci in range(2):
                    q_c = qe[ci * GLA_CHUNK:(ci + 1) * GLA_CHUNK]
                    outs.append(_dot(q_c, state.astype(BF16)))
                    dcol = dect_ref[bi, dk, 2 * cp + ci:2 * cp + ci + 1]
                    state = state * dcol + res[pair + ci * GLA_DK:pair + (ci + 1) * GLA_DK]
                state_ref[bi, hd] = state
                o = res[:pair] + jnp.concatenate(outs, axis=0)
                heads.append(_rms(o, og))
            gt = gate_ref[bi, rows, :]
            o_ref[bi, rows, :] = (jnp.concatenate(heads, axis=1) * (gt * jax.nn.sigmoid(gt))).astype(BF16)


def _gla(qe, ke, klt_lo, klt_hi, v, gate, dect, j, w, *, batch, seq, ts):
    nb = 2
    nt = seq // ts
    r3 = lambda a: a.reshape(batch, seq, a.shape[-1])
    blk = lambda width: pl.BlockSpec((nb, ts, width), lambda p, t: (p, t, 0))
    klt = lambda bi: pl.BlockSpec((GLA_DQK, ts), lambda p, t: (0, (nb * p + bi) * nt + t))
    out = pl.pallas_call(
        _gla_kernel,
        grid=(batch // nb, nt),
        in_specs=[blk(GLA_DQK), blk(GLA_DQK), klt(0), klt(1), klt(0), klt(1), blk(GLA_DVS), blk(GLA_DVS),
                  pl.BlockSpec((nb, GLA_DQK, DEC_LANES), lambda p, t: (p, t, 0)), _layer_spec(w["od_og"], j)],
        out_specs=blk(GLA_DVS),
        out_shape=jax.ShapeDtypeStruct((batch, seq, GLA_DVS), BF16),
        scratch_shapes=[pltpu.VMEM((nb, GLA_HEADS, GLA_DK, GLA_DV), F32)],
        compiler_params=_params("arbitrary", "arbitrary"),
        name="gla",
    )(r3(qe), r3(ke), klt_lo, klt_lo, klt_hi, klt_hi, r3(v), r3(gate),
      dect.reshape(batch, nt * GLA_DQK, DEC_LANES), w["od_og"])
    return out.reshape(batch * seq, GLA_DVS)


def _mix_mlp_kernel(x_ref, ma_ref, mb_ref, wo_ref, g_ref, w1_ref, w2_ref, fg_ref, o_ref, *, final_norm):
    y = x_ref[...] + _dot(jnp.concatenate([ma_ref[...], mb_ref[...]], axis=1), wo_ref[...])
    h = _rms(y, g_ref[...]).astype(BF16)
    acc = y
    for c0 in range(0, D_FF, FF_CHUNK):
        u = _dot(h, w1_ref[:, c0:c0 + FF_CHUNK])
        u = jnp.square(jnp.maximum(u, 0.0)).astype(BF16)
        acc = acc + _dot(u, w2_ref[c0:c0 + FF_CHUNK, :])
    if final_norm:
        acc = _rms(acc, fg_ref[...])
    o_ref[...] = acc


def _mix_mlp(x, mix_a, mix_b, wo, j, layer, w, *, tm, final_norm):
    t = x.shape[0]
    half = D_MODEL // 2
    row = lambda width: pl.BlockSpec((tm, width), lambda i: (i, 0))
    b_spec = row(half) if mix_b is not mix_a else pl.BlockSpec((tm, half), lambda i: (i, 1))
    return pl.pallas_call(
        functools.partial(_mix_mlp_kernel, final_norm=final_norm),
        grid=(t // tm,),
        in_specs=[row(D_MODEL), row(half), b_spec, _layer_spec(wo, j), _layer_spec(w["mlp_g"], layer),
                  _layer_spec(w["mlp_w1"], layer), _layer_spec(w["mlp_w2"], layer),
                  _const_spec(w["final_g"].shape)],
        out_specs=row(D_MODEL),
        out_shape=jax.ShapeDtypeStruct((t, D_MODEL), F32),
        compiler_params=_params("arbitrary"),
        name="mix_mlp",
    )(x, mix_a, mix_b, wo, w["mlp_g"], w["mlp_w1"], w["mlp_w2"], w["final_g"])


def _rope_table(positions):
    inv_freq = 1.0 / (ROPE_THETA ** (jnp.arange(0, ROPE, 2, dtype=F32) / ROPE))
    ang = inv_freq.reshape(-1, 1) * positions.astype(F32).reshape(1, -1)
    cos = jnp.cos(ang)
    sin = jnp.sin(ang)
    return jnp.concatenate([cos, cos, sin, sin], axis=0)


def _prepare_weights(mix_norm_g, mlp_norm_g, final_norm_g, ev_w_in, ev_conv_w, ev_q_norm_g, ev_w_qb,
                     ev_kv_norm_g, ev_w_kvb, ev_w_out, od_w_in, od_w_gate2, od_b_gate2, od_o_norm_g, od_w_out,
                     mlp_w1, mlp_w2):
    cast = lambda a: a.astype(BF16)
    vec = lambda a: a.reshape(a.shape[0], 1, a.shape[1])
    n_even = ev_w_in.shape[0]
    pad_last = lambda a, n: jnp.pad(a, [(0, 0)] * (a.ndim - 1) + [(0, n)])
    head_pad = HEAD_PAD - NOPE - ROPE
    wqb = pad_last(ev_w_qb.reshape(n_even, Q_RANK, N_HEADS, NOPE + ROPE), head_pad)
    wkvb = ev_w_kvb.reshape(n_even, KV_RANK, N_HEADS, NOPE + V_DIM)
    wkb = pad_last(wkvb[..., :NOPE], HEAD_PAD - NOPE)
    wv = wkvb[..., NOPE:]
    odd = (jnp.arange(N_HEADS) % 2 == 1)[None, None, :, None]
    wvb = jnp.where(odd, jnp.pad(wv, ((0, 0), (0, 0), (0, 0), (V_DIM, 0))), pad_last(wv, V_DIM))
    flat = lambda a: a.reshape(a.shape[0], a.shape[1], N_HEADS * HEAD_PAD)
    lane = np.arange(N_HEADS * HEAD_PAD) % (2 * HEAD_PAD)
    v_ones = jnp.asarray(((lane == V_DIM) | (lane == HEAD_PAD)).astype(np.float32)).reshape(1, -1)
    return {
        "mix_g": vec(mix_norm_g), "mlp_g": vec(mlp_norm_g), "final_g": final_norm_g.reshape(1, D_MODEL),
        "ev_in": cast(pad_last(ev_w_in, EVEN_IN_PAD - ev_w_in.shape[-1])), "ev_conv": ev_conv_w,
        "ev_qg": vec(ev_q_norm_g), "ev_qb": cast(flat(wqb)), "ev_kvg": vec(ev_kv_norm_g),
        "ev_kb": cast(flat(wkb)), "ev_vb": cast(flat(wvb)), "v_ones": v_ones, "ev_out": cast(ev_w_out),
        "od_in": cast(pad_last(od_w_in, ODD_IN_PAD - od_w_in.shape[-1])),
        "od_g2": cast(jnp.pad(od_w_gate2, ((0, 0), (0, GATE_PAD - GATE_RANK), (0, 0)))),
        "od_b2": vec(od_b_gate2), "od_og": vec(od_o_norm_g), "od_out": cast(od_w_out),
        "mlp_w1": cast(mlp_w1), "mlp_w2": cast(mlp_w2),
    }


def kernel(x, positions, mix_norm_g, mlp_norm_g, final_norm_g, ev_w_in, ev_conv_w, ev_q_norm_g, ev_w_qb,
           ev_kv_norm_g, ev_w_kvb, ev_w_out, od_w_in, od_w_gate2, od_b_gate2, od_o_norm_g, od_w_out,
           mlp_w1, mlp_w2):
    batch, seq, _ = x.shape
    depth = mix_norm_g.shape[0]
    tm = min(ROW_TILE, seq)
    tq = min(ATTN_Q_TILE, seq)
    tk = min(ATTN_SUB_TILE, tq)
    w = _prepare_weights(mix_norm_g, mlp_norm_g, final_norm_g, ev_w_in, ev_conv_w, ev_q_norm_g, ev_w_qb,
                         ev_kv_norm_g, ev_w_kvb, ev_w_out, od_w_in, od_w_gate2, od_b_gate2, od_o_norm_g,
                         od_w_out, mlp_w1, mlp_w2)
    cs = _rope_table(positions)
    xf = x.reshape(batch * seq, D_MODEL)
    for layer in range(depth):
        j = layer // 2
        if layer % 2 == 0:
            ya, q, k, v = _even_in(xf, j, layer, w, cs, seq=seq, tm=tm)
            yb = _attention(q, k, v, batch=batch, seq=seq, tq=tq, tk=tk)
            mix_a, mix_b, wo = ya, yb, w["ev_out"]
        else:
            qe, ke, klt_lo, klt_hi, v, gate, dect = _odd_in(xf, j, layer, w, tm=tm)
            mix_a = mix_b = _gla(qe, ke, klt_lo, klt_hi, v, gate, dect, j, w, batch=batch, seq=seq, ts=tm)
            wo = w["od_out"]
        xf = _mix_mlp(xf, mix_a, mix_b, wo, j, layer, w, tm=tm, final_norm=(layer == depth - 1))
    return xf.reshape(batch, seq, D_MODEL)
```

```python
import functools

import jax
import jax.numpy as jnp
import numpy as np
from jax import lax
from jax.experimental import pallas as pl
from jax.experimental.pallas import tpu as pltpu

D_MODEL = 1024
CONV_W = 512
N_HEADS = 8
NOPE = 64
ROPE = 32
V_DIM = 64
Q_RANK = 384
KV_RANK = 256
ROPE_THETA = 10000.0
LANES = 128
HEAD_PAD = LANES
EVEN_IN_PAD = 3 * CONV_W + Q_RANK + KV_RANK + LANES
GLA_HEADS = 4
GLA_DK = 128
GLA_DV = 256
GLA_DQK = GLA_HEADS * GLA_DK
GLA_DVS = GLA_HEADS * GLA_DV
GATE_RANK = 16
GATE_PAD = LANES
ODD_IN_PAD = 2 * GLA_DQK + 2 * GLA_DVS + GATE_PAD
GATE_TAU = 16.0
GLA_CHUNK = 64
DEC_LANES = LANES
D_FF = 4 * D_MODEL
EPS = 1e-6

BF16 = jnp.bfloat16
F32 = jnp.float32

VMEM_LIMIT_BYTES = 56 * 1024 * 1024
ROW_TILE = 512
MLP_ROW_TILE = 1024
ATTN_Q_TILE = 2048
ATTN_KV_STEP = 1024
ATTN_SUB_TILE = 512
FF_CHUNK = 512


def _rms(x, g):
    ms = jnp.mean(x * x, axis=-1, keepdims=True)
    return x * lax.rsqrt(ms + EPS) * g


def _dot(a, b):
    return jnp.dot(a, b, preferred_element_type=F32)


def _dot_nt(a, b):
    return lax.dot_general(a, b, (((1,), (1,)), ((), ())), preferred_element_type=F32)


def _const_spec(shape):
    nd = len(shape)
    return pl.BlockSpec(shape, lambda *_: (0,) * nd, pipeline_mode=pl.Buffered(1))


def _layer_spec(arr, j):
    shape = arr.shape[1:]
    nd = len(shape)
    return pl.BlockSpec((None,) + shape, lambda *_: (j,) + (0,) * nd, pipeline_mode=pl.Buffered(1))


def _params(*sem):
    return pltpu.CompilerParams(dimension_semantics=sem, vmem_limit_bytes=VMEM_LIMIT_BYTES)


def _rope(p, c, sa, sbn):
    return p * c + pltpu.roll(p, 16, 1) * sa + pltpu.roll(p, 112, 1) * sbn


def _even_in_kernel(x_ref, g_ref, win_ref, cw_ref, qg_ref, wqb_ref, wqr_ref, kvg_ref, wkb_ref, wvb_ref, vones_ref,
                    cs_ref, ya_ref, q_ref, k_ref, v_ref, carry_ref, *, tiles_per_seq, q_scale):
    tm = x_ref.shape[0]
    i = pl.program_id(0)
    h = _rms(x_ref[...], g_ref[...]).astype(BF16)

    a_c = _dot(h, win_ref[:, CONV_W:2 * CONV_W])
    a_v = _dot(h, win_ref[:, 2 * CONV_W:3 * CONV_W])
    u = a_c * a_v

    @pl.when(i % tiles_per_seq == 0)
    def _():
        carry_ref[...] = jnp.zeros_like(carry_ref)

    prev1 = carry_ref[7:8, :]
    prev2 = carry_ref[6:7, :]
    row = lax.broadcasted_iota(jnp.int32, u.shape, 0)
    u1 = jnp.where(row == 0, prev1, pltpu.roll(u, 1, 0))
    u2 = jnp.where(row == 0, prev2, jnp.where(row == 1, prev1, pltpu.roll(u, 2, 0)))
    carry_ref[...] = u[tm - 8:, :]
    conv = cw_ref[0:1, :] * u2 + cw_ref[1:2, :] * u1 + cw_ref[2:3, :] * u
    a_b = _dot(h, win_ref[:, 0:CONV_W])
    ya_ref[...] = (a_b * conv).astype(BF16)

    cst = cs_ref[...]
    cs = jnp.concatenate([cst, jnp.zeros((LANES - cst.shape[0], tm), F32)], axis=0).T
    lane = lax.broadcasted_iota(jnp.int32, cs.shape, 1)
    half = ROPE // 2
    c = jnp.where(lane < NOPE, 1.0, jnp.where(lane < NOPE + ROPE, pltpu.roll(cs, NOPE, 1), 0.0))
    sin = pltpu.roll(cs, NOPE - ROPE, 1)
    sa = jnp.where((lane >= NOPE + half) & (lane < NOPE + ROPE), sin, 0.0)
    sbn = jnp.where((lane >= NOPE) & (lane < NOPE + half), -sin, 0.0)

    zb = _dot(h, win_ref[:, 3 * CONV_W:])
    qn = _rms(zb[:, :Q_RANK], qg_ref[...]).astype(BF16)
    kvn = _rms(zb[:, Q_RANK:Q_RANK + KV_RANK], kvg_ref[...]).astype(BF16)
    k_pe = _rope(pltpu.roll(zb[:, Q_RANK + KV_RANK:], NOPE, 1), c, sa, sbn)
    qp = _dot(qn, wqb_ref[...])
    qr = _dot(qn, wqr_ref[...])
    cq = c * q_scale
    sq = jnp.where((lane >= NOPE) & (lane < NOPE + ROPE), sin, 0.0) * q_scale
    for hd in range(N_HEADS):
        sl = slice(hd * HEAD_PAD, (hd + 1) * HEAD_PAD)
        q_ref[:, sl] = (qp[:, sl] * cq + qr[:, sl] * sq).astype(BF16)
    kp = _dot(kvn, wkb_ref[...])
    for hd in range(N_HEADS):
        sl = slice(hd * HEAD_PAD, (hd + 1) * HEAD_PAD)
        k_ref[:, sl] = (kp[:, sl] + k_pe).astype(BF16)
    v_ref[...] = (_dot(kvn, wvb_ref[...]) + vones_ref[...]).astype(BF16)


def _even_in(x, j, layer, w, cs, *, seq, tm):
    t = x.shape[0]
    row = lambda width: pl.BlockSpec((tm, width), lambda i: (i, 0))
    q_scale = float((NOPE + ROPE) ** -0.5 * np.log2(np.e))
    wide = jax.ShapeDtypeStruct((t, N_HEADS * HEAD_PAD), BF16)
    return pl.pallas_call(
        functools.partial(_even_in_kernel, tiles_per_seq=seq // tm, q_scale=q_scale),
        grid=(t // tm,),
        in_specs=[row(D_MODEL), _layer_spec(w["mix_g"], layer), _layer_spec(w["ev_in"], j),
                  _layer_spec(w["ev_conv"], j), _layer_spec(w["ev_qg"], j), _layer_spec(w["ev_qb"], j),
                  _layer_spec(w["ev_qr"], j),
                  _layer_spec(w["ev_kvg"], j), _layer_spec(w["ev_kb"], j), _layer_spec(w["ev_vb"], j),
                  _const_spec(w["v_ones"].shape), pl.BlockSpec((2 * ROPE, tm), lambda i: (0, i))],
        out_specs=[row(CONV_W), row(N_HEADS * HEAD_PAD), row(N_HEADS * HEAD_PAD), row(N_HEADS * HEAD_PAD)],
        out_shape=[jax.ShapeDtypeStruct((t, CONV_W), BF16), wide, wide, wide],
        scratch_shapes=[pltpu.VMEM((8, CONV_W), F32)],
        compiler_params=_params("arbitrary"),
        name="even_in",
    )(x, w["mix_g"], w["ev_in"], w["ev_conv"], w["ev_qg"], w["ev_qb"], w["ev_qr"], w["ev_kvg"], w["ev_kb"], w["ev_vb"],
      w["v_ones"], cs)


def _attn_kernel(q_ref, k_ref, v_ref, o_ref, *, tk, kv_step, n_q):
    tq = q_ref.shape[0]
    n_sub = tq // tk
    qi = pl.program_id(2)

    def update(state, hd, r0, rows, start, width, diag):
        lanes = slice(hd * HEAD_PAD, (hd + 1) * HEAD_PAD)
        s = _dot_nt(q_ref[r0:r0 + rows, lanes], k_ref[start:start + width, lanes])
        if diag:
            row = lax.broadcasted_iota(jnp.int32, (rows, width), 0)
            col = lax.broadcasted_iota(jnp.int32, (rows, width), 1)
            s = jnp.where(col <= row + (width - rows), s, -jnp.inf)
        m_blk = jnp.max(s, axis=-1, keepdims=True)
        if state is None:
            m_new = m_blk
        else:
            m, acc = state
            m_new = jnp.maximum(m, m_blk)
        pv = _dot(jnp.exp2(s - m_new).astype(BF16), v_ref[start:start + width, lanes])
        if state is not None:
            pv = jnp.exp2(m - m_new) * acc + pv
        return m_new, pv

    def tile_body(c):
        states = [None, None]
        for start in range(0, c * tq, kv_step):
            states = [update(states[hd], hd, 0, tq, start, kv_step, False) for hd in range(2)]
        lane = lax.broadcasted_iota(jnp.int32, (tk, HEAD_PAD), 1)
        for a in range(n_sub):
            rows = slice(a * tk, (a + 1) * tk)
            accs = []
            for hd in range(2):
                st = None if states[hd] is None else (states[hd][0][rows], states[hd][1][rows])
                accs.append(update(st, hd, a * tk, tk, c * tq, (a + 1) * tk, True)[1])
            o0 = accs[0] / accs[0][:, V_DIM:V_DIM + 1]
            o1 = accs[1] / accs[1][:, 0:1]
            o_ref[rows, :] = jnp.where(lane < V_DIM, o0, o1).astype(BF16)

    for c in range(n_q):
        pl.when(qi == c)(functools.partial(tile_body, c))


def _attention(q, k, v, *, batch, seq, tq, tk):
    t = q.shape[0]
    nq = seq // tq
    return pl.pallas_call(
        functools.partial(_attn_kernel, tk=tk, kv_step=min(ATTN_KV_STEP, tq), n_q=nq),
        grid=(batch, N_HEADS // 2, nq),
        in_specs=[pl.BlockSpec((tq, 2 * HEAD_PAD), lambda b, p, i: (b * nq + i, p)),
                  pl.BlockSpec((seq, 2 * HEAD_PAD), lambda b, p, i: (b, p)),
                  pl.BlockSpec((seq, 2 * HEAD_PAD), lambda b, p, i: (b, p))],
        out_specs=pl.BlockSpec((tq, 2 * V_DIM), lambda b, p, i: (b * nq + i, p)),
        out_shape=jax.ShapeDtypeStruct((t, N_HEADS * V_DIM), BF16),
        compiler_params=_params("arbitrary", "arbitrary", "arbitrary"),
        name="attention",
    )(q, k, v)


def _chunk_cumsum(x):
    pos = lax.broadcasted_iota(jnp.int32, x.shape, 0) & (GLA_CHUNK - 1)
    d = 1
    while d < GLA_CHUNK:
        x = x + jnp.where(pos >= d, pltpu.roll(x, d, 0), 0.0)
        d *= 2
    return x


def _odd_in_kernel(x_ref, g_ref, win_ref, wg2_ref, bg2_ref,
                   qe_ref, ke_ref, klt_lo_ref, klt_hi_ref, v_ref, gate_ref, dect_ref, *, q_scale):
    tm = x_ref.shape[0]
    nc = tm // GLA_CHUNK
    q0, k0, v0, g0, l0 = 0, GLA_DQK, 2 * GLA_DQK, 2 * GLA_DQK + GLA_DVS, 2 * GLA_DQK + 2 * GLA_DVS
    h = _rms(x_ref[...], g_ref[...]).astype(BF16)
    g_low = _dot(h, win_ref[:, l0:]).astype(BF16)
    gate_in = _dot(g_low, wg2_ref[...]) + bg2_ref[...]
    log_a = (jnp.minimum(gate_in, 0.0) - jnp.log1p(jnp.exp(-jnp.abs(gate_in)))) * (1.0 / GATE_TAU)
    bc = _chunk_cumsum(log_a)
    q = _dot(h, win_ref[:, q0:k0])
    qe_ref[...] = (q * q_scale * jnp.exp(bc)).astype(BF16)
    k = _dot(h, win_ref[:, k0:v0])
    ke_ref[...] = (k * jnp.exp(-bc)).astype(BF16)
    kl_parts = []
    dec_parts = []
    for ci in range(nc):
        rows = slice(ci * GLA_CHUNK, (ci + 1) * GLA_CHUNK)
        b_last = bc[(ci + 1) * GLA_CHUNK - 1:(ci + 1) * GLA_CHUNK, :]
        dec_parts.append(jnp.exp(b_last))
        kl_parts.append(k[rows, :] * jnp.exp(b_last - bc[rows, :]))
    klt = jnp.concatenate(kl_parts, axis=0).T
    odd_chunk = (lax.broadcasted_iota(jnp.int32, klt.shape, 1) & GLA_CHUNK) != 0
    klt_lo_ref[...] = jnp.where(odd_chunk, 0.0, klt).astype(BF16)
    klt_hi_ref[...] = jnp.where(odd_chunk, klt, 0.0).astype(BF16)
    dec_pad = jnp.concatenate(dec_parts + [jnp.zeros((DEC_LANES - nc, GLA_DQK), F32)], axis=0)
    dect_ref[...] = dec_pad.T
    v_ref[...] = _dot(h, win_ref[:, v0:g0]).astype(BF16)
    gate_ref[...] = _dot(h, win_ref[:, g0:l0])


def _odd_in(x, j, layer, w, *, tm):
    t = x.shape[0]
    row = lambda width: pl.BlockSpec((tm, width), lambda i: (i, 0))
    col = pl.BlockSpec((GLA_DQK, tm), lambda i: (0, i))
    return pl.pallas_call(
        functools.partial(_odd_in_kernel, q_scale=float(GLA_DK ** -0.5)),
        grid=(t // tm,),
        in_specs=[row(D_MODEL), _layer_spec(w["mix_g"], layer), _layer_spec(w["od_in"], j),
                  _layer_spec(w["od_g2"], j), _layer_spec(w["od_b2"], j)],
        out_specs=[row(GLA_DQK), row(GLA_DQK), col, col, row(GLA_DVS), row(GLA_DVS),
                   pl.BlockSpec((GLA_DQK, DEC_LANES), lambda i: (i, 0))],
        out_shape=[jax.ShapeDtypeStruct((t, GLA_DQK), BF16), jax.ShapeDtypeStruct((t, GLA_DQK), BF16),
                   jax.ShapeDtypeStruct((GLA_DQK, t), BF16), jax.ShapeDtypeStruct((GLA_DQK, t), BF16),
                   jax.ShapeDtypeStruct((t, GLA_DVS), BF16), jax.ShapeDtypeStruct((t, GLA_DVS), F32),
                   jax.ShapeDtypeStruct((t // tm * GLA_DQK, DEC_LANES), F32)],
        compiler_params=_params("arbitrary"),
        name="odd_in",
    )(x, w["mix_g"], w["od_in"], w["od_g2"], w["od_b2"])


def _gla_kernel(qe_ref, ke_ref, lo0_ref, lo1_ref, hi0_ref, hi1_ref, v_ref, gate_ref, dect_ref, og_ref,
                o_ref, state_ref):
    nb, ts, _ = qe_ref.shape
    pair = 2 * GLA_CHUNK
    klt_refs = ((lo0_ref, hi0_ref), (lo1_ref, hi1_ref))

    @pl.when(pl.program_id(1) == 0)
    def _():
        state_ref[...] = jnp.zeros_like(state_ref)

    r = lax.broadcasted_iota(jnp.int32, (pair, pair), 0)
    c = lax.broadcasted_iota(jnp.int32, (pair, pair), 1)
    intra = (c <= r) & ((r < GLA_CHUNK) | (c >= GLA_CHUNK))
    og = og_ref[...]
    for cp in range(ts // pair):
        rows = slice(cp * pair, (cp + 1) * pair)
        for bi in range(nb):
            heads = []
            for hd in range(GLA_HEADS):
                dk = slice(hd * GLA_DK, (hd + 1) * GLA_DK)
                dv = slice(hd * GLA_DV, (hd + 1) * GLA_DV)
                qe = qe_ref[bi, rows, dk]
                vv = v_ref[bi, rows, dv]
                a = jnp.where(intra, _dot_nt(qe, ke_ref[bi, rows, dk]), 0.0).astype(BF16)
                lhs = jnp.concatenate([a, klt_refs[bi][0][dk, rows], klt_refs[bi][1][dk, rows]], axis=0)
                res = _dot(lhs, vv)
                state = state_ref[bi, hd]
                outs = []
                for ci in range(2):
                    q_c = qe[ci * GLA_CHUNK:(ci + 1) * GLA_CHUNK]
                    outs.append(_dot(q_c, state.astype(BF16)))
                    dcol = dect_ref[bi, dk, 2 * cp + ci:2 * cp + ci + 1]
                    state = state * dcol + res[pair + ci * GLA_DK:pair + (ci + 1) * GLA_DK]
                state_ref[bi, hd] = state
                o = res[:pair] + jnp.concatenate(outs, axis=0)
                heads.append(_rms(o, og))
            gt = gate_ref[bi, rows, :]
            o_ref[bi, rows, :] = (jnp.concatenate(heads, axis=1) * (gt * jax.nn.sigmoid(gt))).astype(BF16)


def _gla(qe, ke, klt_lo, klt_hi, v, gate, dect, j, w, *, batch, seq, ts):
    nb = 2
    nt = seq // ts
    r3 = lambda a: a.reshape(batch, seq, a.shape[-1])
    blk = lambda width: pl.BlockSpec((nb, ts, width), lambda p, t: (p, t, 0))
    klt = lambda bi: pl.BlockSpec((GLA_DQK, ts), lambda p, t: (0, (nb * p + bi) * nt + t))
    out = pl.pallas_call(
        _gla_kernel,
        grid=(batch // nb, nt),
        in_specs=[blk(GLA_DQK), blk(GLA_DQK), klt(0), klt(1), klt(0), klt(1), blk(GLA_DVS), blk(GLA_DVS),
                  pl.BlockSpec((nb, GLA_DQK, DEC_LANES), lambda p, t: (p, t, 0)), _layer_spec(w["od_og"], j)],
        out_specs=blk(GLA_DVS),
        out_shape=jax.ShapeDtypeStruct((batch, seq, GLA_DVS), BF16),
        scratch_shapes=[pltpu.VMEM((nb, GLA_HEADS, GLA_DK, GLA_DV), F32)],
        compiler_params=_params("arbitrary", "arbitrary"),
        name="gla",
    )(r3(qe), r3(ke), klt_lo, klt_lo, klt_hi, klt_hi, r3(v), r3(gate),
      dect.reshape(batch, nt * GLA_DQK, DEC_LANES), w["od_og"])
    return out.reshape(batch * seq, GLA_DVS)


def _mix_mlp_kernel(x_ref, ma_ref, mb_ref, wo_ref, g_ref, w1_ref, w2_ref, fg_ref, o_ref, *, final_norm):
    y = x_ref[...] + _dot(jnp.concatenate([ma_ref[...], mb_ref[...]], axis=1), wo_ref[...])
    h = _rms(y, g_ref[...]).astype(BF16)
    acc = y
    for c0 in range(0, D_FF, FF_CHUNK):
        u = _dot(h, w1_ref[:, c0:c0 + FF_CHUNK])
        u = jnp.square(jnp.maximum(u, 0.0)).astype(BF16)
        acc = acc + _dot(u, w2_ref[c0:c0 + FF_CHUNK, :])
    if final_norm:
        acc = _rms(acc, fg_ref[...])
    o_ref[...] = acc


def _mix_mlp(x, mix_a, mix_b, wo, j, layer, w, *, tm, final_norm):
    t = x.shape[0]
    half = D_MODEL // 2
    row = lambda width: pl.BlockSpec((tm, width), lambda i: (i, 0))
    b_spec = row(half) if mix_b is not mix_a else pl.BlockSpec((tm, half), lambda i: (i, 1))
    return pl.pallas_call(
        functools.partial(_mix_mlp_kernel, final_norm=final_norm),
        grid=(t // tm,),
        in_specs=[row(D_MODEL), row(half), b_spec, _layer_spec(wo, j), _layer_spec(w["mlp_g"], layer),
                  _layer_spec(w["mlp_w1"], layer), _layer_spec(w["mlp_w2"], layer),
                  _const_spec(w["final_g"].shape)],
        out_specs=row(D_MODEL),
        out_shape=jax.ShapeDtypeStruct((t, D_MODEL), F32),
        compiler_params=_params("arbitrary"),
        name="mix_mlp",
    )(x, mix_a, mix_b, wo, w["mlp_g"], w["mlp_w1"], w["mlp_w2"], w["final_g"])


def _rope_table(positions):
    inv_freq = 1.0 / (ROPE_THETA ** (jnp.arange(0, ROPE, 2, dtype=F32) / ROPE))
    ang = inv_freq.reshape(-1, 1) * positions.astype(F32).reshape(1, -1)
    cos = jnp.cos(ang)
    sin = jnp.sin(ang)
    return jnp.concatenate([cos, cos, sin, sin], axis=0)


def _prepare_weights(mix_norm_g, mlp_norm_g, final_norm_g, ev_w_in, ev_conv_w, ev_q_norm_g, ev_w_qb,
                     ev_kv_norm_g, ev_w_kvb, ev_w_out, od_w_in, od_w_gate2, od_b_gate2, od_o_norm_g, od_w_out,
                     mlp_w1, mlp_w2):
    cast = lambda a: a.astype(BF16)
    vec = lambda a: a.reshape(a.shape[0], 1, a.shape[1])
    n_even = ev_w_in.shape[0]
    pad_last = lambda a, n: jnp.pad(a, [(0, 0)] * (a.ndim - 1) + [(0, n)])
    head_pad = HEAD_PAD - NOPE - ROPE
    wqb = pad_last(ev_w_qb.reshape(n_even, Q_RANK, N_HEADS, NOPE + ROPE), head_pad)
    half = ROPE // 2
    wqr = jnp.concatenate([jnp.zeros_like(wqb[..., :NOPE]), -wqb[..., NOPE + half:NOPE + ROPE],
                           wqb[..., NOPE:NOPE + half], jnp.zeros_like(wqb[..., NOPE + ROPE:])], axis=-1)
    wkvb = ev_w_kvb.reshape(n_even, KV_RANK, N_HEADS, NOPE + V_DIM)
    wkb = pad_last(wkvb[..., :NOPE], HEAD_PAD - NOPE)
    wv = wkvb[..., NOPE:]
    odd = (jnp.arange(N_HEADS) % 2 == 1)[None, None, :, None]
    wvb = jnp.where(odd, jnp.pad(wv, ((0, 0), (0, 0), (0, 0), (V_DIM, 0))), pad_last(wv, V_DIM))
    flat = lambda a: a.reshape(a.shape[0], a.shape[1], N_HEADS * HEAD_PAD)
    lane = np.arange(N_HEADS * HEAD_PAD) % (2 * HEAD_PAD)
    v_ones = jnp.asarray(((lane == V_DIM) | (lane == HEAD_PAD)).astype(np.float32)).reshape(1, -1)
    return {
        "mix_g": vec(mix_norm_g), "mlp_g": vec(mlp_norm_g), "final_g": final_norm_g.reshape(1, D_MODEL),
        "ev_in": cast(pad_last(ev_w_in, EVEN_IN_PAD - ev_w_in.shape[-1])), "ev_conv": ev_conv_w,
        "ev_qg": vec(ev_q_norm_g), "ev_qb": cast(flat(wqb)), "ev_qr": cast(flat(wqr)), "ev_kvg": vec(ev_kv_norm_g),
        "ev_kb": cast(flat(wkb)), "ev_vb": cast(flat(wvb)), "v_ones": v_ones, "ev_out": cast(ev_w_out),
        "od_in": cast(pad_last(od_w_in, ODD_IN_PAD - od_w_in.shape[-1])),
        "od_g2": cast(jnp.pad(od_w_gate2, ((0, 0), (0, GATE_PAD - GATE_RANK), (0, 0)))),
        "od_b2": vec(od_b_gate2), "od_og": vec(od_o_norm_g), "od_out": cast(od_w_out),
        "mlp_w1": cast(mlp_w1), "mlp_w2": cast(mlp_w2),
    }


def kernel(x, positions, mix_norm_g, mlp_norm_g, final_norm_g, ev_w_in, ev_conv_w, ev_q_norm_g, ev_w_qb,
           ev_kv_norm_g, ev_w_kvb, ev_w_out, od_w_in, od_w_gate2, od_b_gate2, od_o_norm_g, od_w_out,
           mlp_w1, mlp_w2):
    batch, seq, _ = x.shape
    depth = mix_norm_g.shape[0]
    tm = min(ROW_TILE, seq)
    tq = min(ATTN_Q_TILE, seq)
    tk = min(ATTN_SUB_TILE, tq)
    w = _prepare_weights(mix_norm_g, mlp_norm_g, final_norm_g, ev_w_in, ev_conv_w, ev_q_norm_g, ev_w_qb,
                         ev_kv_norm_g, ev_w_kvb, ev_w_out, od_w_in, od_w_gate2, od_b_gate2, od_o_norm_g,
                         od_w_out, mlp_w1, mlp_w2)
    cs = _rope_table(positions)
    xf = x.reshape(batch * seq, D_MODEL)
    for layer in range(depth):
        j = layer // 2
        if layer % 2 == 0:
            ya, q, k, v = _even_in(xf, j, layer, w, cs, seq=seq, tm=tm)
            yb = _attention(q, k, v, batch=batch, seq=seq, tq=tq, tk=tk)
            mix_a, mix_b, wo = ya, yb, w["ev_out"]
        else:
            qe, ke, klt_lo, klt_hi, v, gate, dect = _odd_in(xf, j, layer, w, tm=tm)
            mix_a = mix_b = _gla(qe, ke, klt_lo, klt_hi, v, gate, dect, j, w, batch=batch, seq=seq, ts=tm)
            wo = w["od_out"]
        xf = _mix_mlp(xf, mix_a, mix_b, wo, j, layer, w, tm=min(MLP_ROW_TILE, seq),
                      final_norm=(layer == depth - 1))
    return xf.reshape(batch, seq, D_MODEL)
```

```python
import functools

import jax
import jax.numpy as jnp
import numpy as np
from jax import lax
from jax.experimental import pallas as pl
from jax.experimental.pallas import tpu as pltpu

D_MODEL = 1024
CONV_W = 512
N_HEADS = 8
NOPE = 64
ROPE = 32
V_DIM = 64
Q_RANK = 384
KV_RANK = 256
ROPE_THETA = 10000.0
LANES = 128
HEAD_PAD = LANES
EVEN_IN_PAD = 3 * CONV_W + Q_RANK + KV_RANK + LANES
GLA_HEADS = 4
GLA_DK = 128
GLA_DV = 256
GLA_DQK = GLA_HEADS * GLA_DK
GLA_DVS = GLA_HEADS * GLA_DV
GATE_RANK = 16
GATE_PAD = LANES
ODD_IN_PAD = 2 * GLA_DQK + 2 * GLA_DVS + GATE_PAD
GATE_TAU = 16.0
GLA_CHUNK = 64
DEC_LANES = LANES
D_FF = 4 * D_MODEL
EPS = 1e-6

BF16 = jnp.bfloat16
F32 = jnp.float32

VMEM_LIMIT_BYTES = 56 * 1024 * 1024
ROW_TILE = 1024
GLA_TILE = 512
MLP_ROW_TILE = 1024
ATTN_Q_TILE = 2048
ATTN_KV_STEP = 1024
ATTN_SUB_TILE = 512
FF_CHUNK = 512


def _rms(x, g):
    ms = jnp.mean(x * x, axis=-1, keepdims=True)
    return x * lax.rsqrt(ms + EPS) * g


def _dot(a, b):
    return jnp.dot(a, b, preferred_element_type=F32)


def _dot_nt(a, b):
    return lax.dot_general(a, b, (((1,), (1,)), ((), ())), preferred_element_type=F32)


def _const_spec(shape):
    nd = len(shape)
    return pl.BlockSpec(shape, lambda *_: (0,) * nd, pipeline_mode=pl.Buffered(1))


def _layer_spec(arr, j):
    shape = arr.shape[1:]
    nd = len(shape)
    return pl.BlockSpec((None,) + shape, lambda *_: (j,) + (0,) * nd, pipeline_mode=pl.Buffered(1))


def _params(*sem):
    return pltpu.CompilerParams(dimension_semantics=sem, vmem_limit_bytes=VMEM_LIMIT_BYTES)


def _rope(p, c, sa, sbn):
    return p * c + pltpu.roll(p, 16, 1) * sa + pltpu.roll(p, 112, 1) * sbn


def _even_in_kernel(x_ref, g_ref, win_ref, cw_ref, qg_ref, wqb_ref, wqr_ref, kvg_ref, wkb_ref, wvb_ref, vones_ref,
                    cs_ref, ya_ref, q_ref, k_ref, v_ref, carry_ref, *, tiles_per_seq, q_scale):
    tm = x_ref.shape[0]
    i = pl.program_id(0)
    h = _rms(x_ref[...], g_ref[...]).astype(BF16)

    a_c = _dot(h, win_ref[:, CONV_W:2 * CONV_W])
    a_v = _dot(h, win_ref[:, 2 * CONV_W:3 * CONV_W])
    u = a_c * a_v

    @pl.when(i % tiles_per_seq == 0)
    def _():
        carry_ref[...] = jnp.zeros_like(carry_ref)

    prev1 = carry_ref[7:8, :]
    prev2 = carry_ref[6:7, :]
    row = lax.broadcasted_iota(jnp.int32, u.shape, 0)
    u1 = jnp.where(row == 0, prev1, pltpu.roll(u, 1, 0))
    u2 = jnp.where(row == 0, prev2, jnp.where(row == 1, prev1, pltpu.roll(u, 2, 0)))
    carry_ref[...] = u[tm - 8:, :]
    conv = cw_ref[0:1, :] * u2 + cw_ref[1:2, :] * u1 + cw_ref[2:3, :] * u
    a_b = _dot(h, win_ref[:, 0:CONV_W])
    ya_ref[...] = (a_b * conv).astype(BF16)

    cst = cs_ref[...]
    cs = jnp.concatenate([cst, jnp.zeros((LANES - cst.shape[0], tm), F32)], axis=0).T
    lane = lax.broadcasted_iota(jnp.int32, cs.shape, 1)
    half = ROPE // 2
    c = jnp.where(lane < NOPE, 1.0, jnp.where(lane < NOPE + ROPE, pltpu.roll(cs, NOPE, 1), 0.0))
    sin = pltpu.roll(cs, NOPE - ROPE, 1)
    sa = jnp.where((lane >= NOPE + half) & (lane < NOPE + ROPE), sin, 0.0)
    sbn = jnp.where((lane >= NOPE) & (lane < NOPE + half), -sin, 0.0)

    zb = _dot(h, win_ref[:, 3 * CONV_W:])
    qn = _rms(zb[:, :Q_RANK], qg_ref[...]).astype(BF16)
    kvn = _rms(zb[:, Q_RANK:Q_RANK + KV_RANK], kvg_ref[...]).astype(BF16)
    k_pe = _rope(pltpu.roll(zb[:, Q_RANK + KV_RANK:], NOPE, 1), c, sa, sbn)
    qp = _dot(qn, wqb_ref[...])
    qr = _dot(qn, wqr_ref[...])
    cq = c * q_scale
    sq = jnp.where((lane >= NOPE) & (lane < NOPE + ROPE), sin, 0.0) * q_scale
    for hd in range(N_HEADS):
        sl = slice(hd * HEAD_PAD, (hd + 1) * HEAD_PAD)
        q_ref[:, sl] = (qp[:, sl] * cq + qr[:, sl] * sq).astype(BF16)
    kp = _dot(kvn, wkb_ref[...])
    for hd in range(N_HEADS):
        sl = slice(hd * HEAD_PAD, (hd + 1) * HEAD_PAD)
        k_ref[:, sl] = (kp[:, sl] + k_pe).astype(BF16)
    v_ref[...] = (_dot(kvn, wvb_ref[...]) + vones_ref[...]).astype(BF16)


def _even_in(x, j, layer, w, cs, *, seq, tm):
    t = x.shape[0]
    row = lambda width: pl.BlockSpec((tm, width), lambda i: (i, 0))
    q_scale = float((NOPE + ROPE) ** -0.5 * np.log2(np.e))
    wide = jax.ShapeDtypeStruct((t, N_HEADS * HEAD_PAD), BF16)
    return pl.pallas_call(
        functools.partial(_even_in_kernel, tiles_per_seq=seq // tm, q_scale=q_scale),
        grid=(t // tm,),
        in_specs=[row(D_MODEL), _layer_spec(w["mix_g"], layer), _layer_spec(w["ev_in"], j),
                  _layer_spec(w["ev_conv"], j), _layer_spec(w["ev_qg"], j), _layer_spec(w["ev_qb"], j),
                  _layer_spec(w["ev_qr"], j),
                  _layer_spec(w["ev_kvg"], j), _layer_spec(w["ev_kb"], j), _layer_spec(w["ev_vb"], j),
                  _const_spec(w["v_ones"].shape), pl.BlockSpec((2 * ROPE, tm), lambda i: (0, i))],
        out_specs=[row(CONV_W), row(N_HEADS * HEAD_PAD), row(N_HEADS * HEAD_PAD), row(N_HEADS * HEAD_PAD)],
        out_shape=[jax.ShapeDtypeStruct((t, CONV_W), BF16), wide, wide, wide],
        scratch_shapes=[pltpu.VMEM((8, CONV_W), F32)],
        compiler_params=_params("arbitrary"),
        name="even_in",
    )(x, w["mix_g"], w["ev_in"], w["ev_conv"], w["ev_qg"], w["ev_qb"], w["ev_qr"], w["ev_kvg"], w["ev_kb"], w["ev_vb"],
      w["v_ones"], cs)


def _attn_kernel(q_ref, k_ref, v_ref, o_ref, *, tk, kv_step, n_q):
    tq = q_ref.shape[0]
    n_sub = tq // tk
    qi = pl.program_id(2)

    def update(state, hd, r0, rows, start, width, diag):
        lanes = slice(hd * HEAD_PAD, (hd + 1) * HEAD_PAD)
        s = _dot_nt(q_ref[r0:r0 + rows, lanes], k_ref[start:start + width, lanes])
        if diag:
            row = lax.broadcasted_iota(jnp.int32, (rows, width), 0)
            col = lax.broadcasted_iota(jnp.int32, (rows, width), 1)
            s = jnp.where(col <= row + (width - rows), s, -jnp.inf)
        m_blk = jnp.max(s, axis=-1, keepdims=True)
        if state is None:
            m_new = m_blk
        else:
            m, acc = state
            m_new = jnp.maximum(m, m_blk)
        pv = _dot(jnp.exp2(s - m_new).astype(BF16), v_ref[start:start + width, lanes])
        if state is not None:
            pv = jnp.exp2(m - m_new) * acc + pv
        return m_new, pv

    def tile_body(c):
        states = [None, None]
        for start in range(0, c * tq, kv_step):
            states = [update(states[hd], hd, 0, tq, start, kv_step, False) for hd in range(2)]
        lane = lax.broadcasted_iota(jnp.int32, (tk, HEAD_PAD), 1)
        for a in range(n_sub):
            rows = slice(a * tk, (a + 1) * tk)
            accs = []
            for hd in range(2):
                st = None if states[hd] is None else (states[hd][0][rows], states[hd][1][rows])
                accs.append(update(st, hd, a * tk, tk, c * tq, (a + 1) * tk, True)[1])
            o0 = accs[0] / accs[0][:, V_DIM:V_DIM + 1]
            o1 = accs[1] / accs[1][:, 0:1]
            o_ref[rows, :] = jnp.where(lane < V_DIM, o0, o1).astype(BF16)

    for c in range(n_q):
        pl.when(qi == c)(functools.partial(tile_body, c))


def _attention(q, k, v, *, batch, seq, tq, tk):
    t = q.shape[0]
    nq = seq // tq
    return pl.pallas_call(
        functools.partial(_attn_kernel, tk=tk, kv_step=min(ATTN_KV_STEP, tq), n_q=nq),
        grid=(batch, N_HEADS // 2, nq),
        in_specs=[pl.BlockSpec((tq, 2 * HEAD_PAD), lambda b, p, i: (b * nq + i, p)),
                  pl.BlockSpec((seq, 2 * HEAD_PAD), lambda b, p, i: (b, p)),
                  pl.BlockSpec((seq, 2 * HEAD_PAD), lambda b, p, i: (b, p))],
        out_specs=pl.BlockSpec((tq, 2 * V_DIM), lambda b, p, i: (b * nq + i, p)),
        out_shape=jax.ShapeDtypeStruct((t, N_HEADS * V_DIM), BF16),
        compiler_params=_params("arbitrary", "arbitrary", "arbitrary"),
        name="attention",
    )(q, k, v)


def _chunk_cumsum(x):
    pos = lax.broadcasted_iota(jnp.int32, x.shape, 0) & (GLA_CHUNK - 1)
    d = 1
    while d < GLA_CHUNK:
        x = x + jnp.where(pos >= d, pltpu.roll(x, d, 0), 0.0)
        d *= 2
    return x


def _odd_in_kernel(x_ref, g_ref, win_ref, wg2_ref, bg2_ref,
                   qe_ref, ke_ref, klt_lo_ref, klt_hi_ref, v_ref, gate_ref, dect_ref, *, q_scale, ts):
    tm = x_ref.shape[0]
    nc = tm // GLA_CHUNK
    q0, k0, v0, g0, l0 = 0, GLA_DQK, 2 * GLA_DQK, 2 * GLA_DQK + GLA_DVS, 2 * GLA_DQK + 2 * GLA_DVS
    h = _rms(x_ref[...], g_ref[...]).astype(BF16)
    g_low = _dot(h, win_ref[:, l0:]).astype(BF16)
    gate_in = _dot(g_low, wg2_ref[...]) + bg2_ref[...]
    log_a = (jnp.minimum(gate_in, 0.0) - jnp.log1p(jnp.exp(-jnp.abs(gate_in)))) * (1.0 / GATE_TAU)
    bc = _chunk_cumsum(log_a)
    q = _dot(h, win_ref[:, q0:k0])
    qe_ref[...] = (q * q_scale * jnp.exp(bc)).astype(BF16)
    k = _dot(h, win_ref[:, k0:v0])
    ke_ref[...] = (k * jnp.exp(-bc)).astype(BF16)
    kl_parts = []
    dec_parts = []
    for ci in range(nc):
        rows = slice(ci * GLA_CHUNK, (ci + 1) * GLA_CHUNK)
        b_last = bc[(ci + 1) * GLA_CHUNK - 1:(ci + 1) * GLA_CHUNK, :]
        dec_parts.append(jnp.exp(b_last))
        kl_parts.append(k[rows, :] * jnp.exp(b_last - bc[rows, :]))
    klt = jnp.concatenate(kl_parts, axis=0).T
    odd_chunk = (lax.broadcasted_iota(jnp.int32, klt.shape, 1) & GLA_CHUNK) != 0
    klt_lo_ref[...] = jnp.where(odd_chunk, 0.0, klt).astype(BF16)
    klt_hi_ref[...] = jnp.where(odd_chunk, klt, 0.0).astype(BF16)
    per = ts // GLA_CHUNK
    for t0 in range(nc // per):
        dec_pad = jnp.concatenate(dec_parts[t0 * per:(t0 + 1) * per] + [jnp.zeros((DEC_LANES - per, GLA_DQK), F32)],
                                  axis=0)
        dect_ref[t0 * GLA_DQK:(t0 + 1) * GLA_DQK, :] = dec_pad.T
    v_ref[...] = _dot(h, win_ref[:, v0:g0]).astype(BF16)
    gate_ref[...] = _dot(h, win_ref[:, g0:l0])


def _odd_in(x, j, layer, w, *, tm, ts):
    t = x.shape[0]
    row = lambda width: pl.BlockSpec((tm, width), lambda i: (i, 0))
    col = pl.BlockSpec((GLA_DQK, tm), lambda i: (0, i))
    return pl.pallas_call(
        functools.partial(_odd_in_kernel, q_scale=float(GLA_DK ** -0.5), ts=ts),
        grid=(t // tm,),
        in_specs=[row(D_MODEL), _layer_spec(w["mix_g"], layer), _layer_spec(w["od_in"], j),
                  _layer_spec(w["od_g2"], j), _layer_spec(w["od_b2"], j)],
        out_specs=[row(GLA_DQK), row(GLA_DQK), col, col, row(GLA_DVS), row(GLA_DVS),
                   pl.BlockSpec((tm // ts * GLA_DQK, DEC_LANES), lambda i: (i, 0))],
        out_shape=[jax.ShapeDtypeStruct((t, GLA_DQK), BF16), jax.ShapeDtypeStruct((t, GLA_DQK), BF16),
                   jax.ShapeDtypeStruct((GLA_DQK, t), BF16), jax.ShapeDtypeStruct((GLA_DQK, t), BF16),
                   jax.ShapeDtypeStruct((t, GLA_DVS), BF16), jax.ShapeDtypeStruct((t, GLA_DVS), F32),
                   jax.ShapeDtypeStruct((t // ts * GLA_DQK, DEC_LANES), F32)],
        compiler_params=_params("arbitrary"),
        name="odd_in",
    )(x, w["mix_g"], w["od_in"], w["od_g2"], w["od_b2"])


def _gla_kernel(qe_ref, ke_ref, lo0_ref, lo1_ref, hi0_ref, hi1_ref, v_ref, gate_ref, dect_ref, og_ref,
                o_ref, state_ref):
    nb, ts, _ = qe_ref.shape
    pair = 2 * GLA_CHUNK
    klt_refs = ((lo0_ref, hi0_ref), (lo1_ref, hi1_ref))

    @pl.when(pl.program_id(1) == 0)
    def _():
        state_ref[...] = jnp.zeros_like(state_ref)

    r = lax.broadcasted_iota(jnp.int32, (pair, pair), 0)
    c = lax.broadcasted_iota(jnp.int32, (pair, pair), 1)
    intra = (c <= r) & ((r < GLA_CHUNK) | (c >= GLA_CHUNK))
    og = og_ref[...]
    for cp in range(ts // pair):
        rows = slice(cp * pair, (cp + 1) * pair)
        for bi in range(nb):
            heads = []
            for hd in range(GLA_HEADS):
                dk = slice(hd * GLA_DK, (hd + 1) * GLA_DK)
                dv = slice(hd * GLA_DV, (hd + 1) * GLA_DV)
                qe = qe_ref[bi, rows, dk]
                vv = v_ref[bi, rows, dv]
                a = jnp.where(intra, _dot_nt(qe, ke_ref[bi, rows, dk]), 0.0).astype(BF16)
                lhs = jnp.concatenate([a, klt_refs[bi][0][dk, rows], klt_refs[bi][1][dk, rows]], axis=0)
                res = _dot(lhs, vv)
                state = state_ref[bi, hd]
                outs = []
                for ci in range(2):
                    q_c = qe[ci * GLA_CHUNK:(ci + 1) * GLA_CHUNK]
                    outs.append(_dot(q_c, state.astype(BF16)))
                    dcol = dect_ref[bi, dk, 2 * cp + ci:2 * cp + ci + 1]
                    state = state * dcol + res[pair + ci * GLA_DK:pair + (ci + 1) * GLA_DK]
                state_ref[bi, hd] = state
                o = res[:pair] + jnp.concatenate(outs, axis=0)
                heads.append(_rms(o, og))
            gt = gate_ref[bi, rows, :]
            o_ref[bi, rows, :] = (jnp.concatenate(heads, axis=1) * (gt * jax.nn.sigmoid(gt))).astype(BF16)


def _gla(qe, ke, klt_lo, klt_hi, v, gate, dect, j, w, *, batch, seq, ts):
    nb = 2
    nt = seq // ts
    r3 = lambda a: a.reshape(batch, seq, a.shape[-1])
    blk = lambda width: pl.BlockSpec((nb, ts, width), lambda p, t: (p, t, 0))
    klt = lambda bi: pl.BlockSpec((GLA_DQK, ts), lambda p, t: (0, (nb * p + bi) * nt + t))
    out = pl.pallas_call(
        _gla_kernel,
        grid=(batch // nb, nt),
        in_specs=[blk(GLA_DQK), blk(GLA_DQK), klt(0), klt(1), klt(0), klt(1), blk(GLA_DVS), blk(GLA_DVS),
                  pl.BlockSpec((nb, GLA_DQK, DEC_LANES), lambda p, t: (p, t, 0)), _layer_spec(w["od_og"], j)],
        out_specs=blk(GLA_DVS),
        out_shape=jax.ShapeDtypeStruct((batch, seq, GLA_DVS), BF16),
        scratch_shapes=[pltpu.VMEM((nb, GLA_HEADS, GLA_DK, GLA_DV), F32)],
        compiler_params=_params("arbitrary", "arbitrary"),
        name="gla",
    )(r3(qe), r3(ke), klt_lo, klt_lo, klt_hi, klt_hi, r3(v), r3(gate),
      dect.reshape(batch, nt * GLA_DQK, DEC_LANES), w["od_og"])
    return out.reshape(batch * seq, GLA_DVS)


def _mix_mlp_kernel(x_ref, ma_ref, mb_ref, wo_ref, g_ref, w1_ref, w2_ref, fg_ref, o_ref, *, final_norm):
    y = x_ref[...] + _dot(jnp.concatenate([ma_ref[...], mb_ref[...]], axis=1), wo_ref[...])
    h = _rms(y, g_ref[...]).astype(BF16)
    acc = y
    for c0 in range(0, D_FF, FF_CHUNK):
        u = _dot(h, w1_ref[:, c0:c0 + FF_CHUNK])
        u = jnp.square(jnp.maximum(u, 0.0)).astype(BF16)
        acc = acc + _dot(u, w2_ref[c0:c0 + FF_CHUNK, :])
    if final_norm:
        acc = _rms(acc, fg_ref[...])
    o_ref[...] = acc


def _mix_mlp(x, mix_a, mix_b, wo, j, layer, w, *, tm, final_norm):
    t = x.shape[0]
    half = D_MODEL // 2
    row = lambda width: pl.BlockSpec((tm, width), lambda i: (i, 0))
    b_spec = row(half) if mix_b is not mix_a else pl.BlockSpec((tm, half), lambda i: (i, 1))
    return pl.pallas_call(
        functools.partial(_mix_mlp_kernel, final_norm=final_norm),
        grid=(t // tm,),
        in_specs=[row(D_MODEL), row(half), b_spec, _layer_spec(wo, j), _layer_spec(w["mlp_g"], layer),
                  _layer_spec(w["mlp_w1"], layer), _layer_spec(w["mlp_w2"], layer),
                  _const_spec(w["final_g"].shape)],
        out_specs=row(D_MODEL),
        out_shape=jax.ShapeDtypeStruct((t, D_MODEL), F32),
        compiler_params=_params("arbitrary"),
        name="mix_mlp",
    )(x, mix_a, mix_b, wo, w["mlp_g"], w["mlp_w1"], w["mlp_w2"], w["final_g"])


def _rope_table(positions):
    inv_freq = 1.0 / (ROPE_THETA ** (jnp.arange(0, ROPE, 2, dtype=F32) / ROPE))
    ang = inv_freq.reshape(-1, 1) * positions.astype(F32).reshape(1, -1)
    cos = jnp.cos(ang)
    sin = jnp.sin(ang)
    return jnp.concatenate([cos, cos, sin, sin], axis=0)


def _prepare_weights(mix_norm_g, mlp_norm_g, final_norm_g, ev_w_in, ev_conv_w, ev_q_norm_g, ev_w_qb,
                     ev_kv_norm_g, ev_w_kvb, ev_w_out, od_w_in, od_w_gate2, od_b_gate2, od_o_norm_g, od_w_out,
                     mlp_w1, mlp_w2):
    cast = lambda a: a.astype(BF16)
    vec = lambda a: a.reshape(a.shape[0], 1, a.shape[1])
    n_even = ev_w_in.shape[0]
    pad_last = lambda a, n: jnp.pad(a, [(0, 0)] * (a.ndim - 1) + [(0, n)])
    head_pad = HEAD_PAD - NOPE - ROPE
    wqb = pad_last(ev_w_qb.reshape(n_even, Q_RANK, N_HEADS, NOPE + ROPE), head_pad)
    half = ROPE // 2
    wqr = jnp.concatenate([jnp.zeros_like(wqb[..., :NOPE]), -wqb[..., NOPE + half:NOPE + ROPE],
                           wqb[..., NOPE:NOPE + half], jnp.zeros_like(wqb[..., NOPE + ROPE:])], axis=-1)
    wkvb = ev_w_kvb.reshape(n_even, KV_RANK, N_HEADS, NOPE + V_DIM)
    wkb = pad_last(wkvb[..., :NOPE], HEAD_PAD - NOPE)
    wv = wkvb[..., NOPE:]
    odd = (jnp.arange(N_HEADS) % 2 == 1)[None, None, :, None]
    wvb = jnp.where(odd, jnp.pad(wv, ((0, 0), (0, 0), (0, 0), (V_DIM, 0))), pad_last(wv, V_DIM))
    flat = lambda a: a.reshape(a.shape[0], a.shape[1], N_HEADS * HEAD_PAD)
    lane = np.arange(N_HEADS * HEAD_PAD) % (2 * HEAD_PAD)
    v_ones = jnp.asarray(((lane == V_DIM) | (lane == HEAD_PAD)).astype(np.float32)).reshape(1, -1)
    return {
        "mix_g": vec(mix_norm_g), "mlp_g": vec(mlp_norm_g), "final_g": final_norm_g.reshape(1, D_MODEL),
        "ev_in": cast(pad_last(ev_w_in, EVEN_IN_PAD - ev_w_in.shape[-1])), "ev_conv": ev_conv_w,
        "ev_qg": vec(ev_q_norm_g), "ev_qb": cast(flat(wqb)), "ev_qr": cast(flat(wqr)), "ev_kvg": vec(ev_kv_norm_g),
        "ev_kb": cast(flat(wkb)), "ev_vb": cast(flat(wvb)), "v_ones": v_ones, "ev_out": cast(ev_w_out),
        "od_in": cast(pad_last(od_w_in, ODD_IN_PAD - od_w_in.shape[-1])),
        "od_g2": cast(jnp.pad(od_w_gate2, ((0, 0), (0, GATE_PAD - GATE_RANK), (0, 0)))),
        "od_b2": vec(od_b_gate2), "od_og": vec(od_o_norm_g), "od_out": cast(od_w_out),
        "mlp_w1": cast(mlp_w1), "mlp_w2": cast(mlp_w2),
    }


def kernel(x, positions, mix_norm_g, mlp_norm_g, final_norm_g, ev_w_in, ev_conv_w, ev_q_norm_g, ev_w_qb,
           ev_kv_norm_g, ev_w_kvb, ev_w_out, od_w_in, od_w_gate2, od_b_gate2, od_o_norm_g, od_w_out,
           mlp_w1, mlp_w2):
    batch, seq, _ = x.shape
    depth = mix_norm_g.shape[0]
    tm = min(ROW_TILE, seq)
    ts = min(GLA_TILE, tm)
    tq = min(ATTN_Q_TILE, seq)
    tk = min(ATTN_SUB_TILE, tq)
    w = _prepare_weights(mix_norm_g, mlp_norm_g, final_norm_g, ev_w_in, ev_conv_w, ev_q_norm_g, ev_w_qb,
                         ev_kv_norm_g, ev_w_kvb, ev_w_out, od_w_in, od_w_gate2, od_b_gate2, od_o_norm_g,
                         od_w_out, mlp_w1, mlp_w2)
    cs = _rope_table(positions)
    xf = x.reshape(batch * seq, D_MODEL)
    for layer in range(depth):
        j = layer // 2
        if layer % 2 == 0:
            ya, q, k, v = _even_in(xf, j, layer, w, cs, seq=seq, tm=tm)
            yb = _attention(q, k, v, batch=batch, seq=seq, tq=tq, tk=tk)
            mix_a, mix_b, wo = ya, yb, w["ev_out"]
        else:
            qe, ke, klt_lo, klt_hi, v, gate, dect = _odd_in(xf, j, layer, w, tm=tm, ts=ts)
            mix_a = mix_b = _gla(qe, ke, klt_lo, klt_hi, v, gate, dect, j, w, batch=batch, seq=seq, ts=ts)
            wo = w["od_out"]
        xf = _mix_mlp(xf, mix_a, mix_b, wo, j, layer, w, tm=min(MLP_ROW_TILE, seq),
                      final_norm=(layer == depth - 1))
    return xf.reshape(batch, seq, D_MODEL)
```

```python
import functools

import jax
import jax.numpy as jnp
import numpy as np
from jax import lax
from jax.experimental import pallas as pl
from jax.experimental.pallas import tpu as pltpu

D_MODEL = 1024
CONV_W = 512
N_HEADS = 8
NOPE = 64
ROPE = 32
V_DIM = 64
Q_RANK = 384
KV_RANK = 256
ROPE_THETA = 10000.0
LANES = 128
HEAD_PAD = LANES
EVEN_IN_PAD = 3 * CONV_W + Q_RANK + KV_RANK + LANES
GLA_HEADS = 4
GLA_DK = 128
GLA_DV = 256
GLA_DQK = GLA_HEADS * GLA_DK
GLA_DVS = GLA_HEADS * GLA_DV
GATE_RANK = 16
GATE_PAD = LANES
ODD_IN_MAIN = 2 * GLA_DQK + 2 * GLA_DVS
GATE_TAU = 16.0
GLA_CHUNK = 64
DEC_LANES = LANES
D_FF = 4 * D_MODEL
EPS = 1e-6

BF16 = jnp.bfloat16
F32 = jnp.float32

VMEM_LIMIT_BYTES = 56 * 1024 * 1024
ROW_TILE = 1024
GLA_TILE = 512
MLP_ROW_TILE = 1024
ATTN_Q_TILE = 2048
ATTN_KV_STEP = 1024
ATTN_SUB_TILE = 512
FF_CHUNK = 512


def _rms(x, g):
    ms = jnp.mean(x * x, axis=-1, keepdims=True)
    return x * lax.rsqrt(ms + EPS) * g


def _dot(a, b):
    return jnp.dot(a, b, preferred_element_type=F32)


def _dot_nt(a, b):
    return lax.dot_general(a, b, (((1,), (1,)), ((), ())), preferred_element_type=F32)


def _const_spec(shape):
    nd = len(shape)
    return pl.BlockSpec(shape, lambda *_: (0,) * nd, pipeline_mode=pl.Buffered(1))


def _layer_spec(arr, j):
    shape = arr.shape[1:]
    nd = len(shape)
    return pl.BlockSpec((None,) + shape, lambda *_: (j,) + (0,) * nd, pipeline_mode=pl.Buffered(1))


def _params(*sem):
    return pltpu.CompilerParams(dimension_semantics=sem, vmem_limit_bytes=VMEM_LIMIT_BYTES)


def _rope(p, c, sa, sbn):
    return p * c + pltpu.roll(p, 16, 1) * sa + pltpu.roll(p, 112, 1) * sbn


def _even_in_kernel(x_ref, g_ref, win_ref, cw_ref, qg_ref, wqb_ref, wqr_ref, kvg_ref, wkb_ref, wvb_ref, vones_ref,
                    cs_ref, ya_ref, q_ref, k_ref, v_ref, carry_ref, *, tiles_per_seq, q_scale):
    tm = x_ref.shape[0]
    i = pl.program_id(0)
    h = _rms(x_ref[...], g_ref[...]).astype(BF16)

    a_c = _dot(h, win_ref[:, CONV_W:2 * CONV_W])
    a_v = _dot(h, win_ref[:, 2 * CONV_W:3 * CONV_W])
    u = a_c * a_v

    @pl.when(i % tiles_per_seq == 0)
    def _():
        carry_ref[...] = jnp.zeros_like(carry_ref)

    prev1 = carry_ref[7:8, :]
    prev2 = carry_ref[6:7, :]
    row = lax.broadcasted_iota(jnp.int32, u.shape, 0)
    u1 = jnp.where(row == 0, prev1, pltpu.roll(u, 1, 0))
    u2 = jnp.where(row == 0, prev2, jnp.where(row == 1, prev1, pltpu.roll(u, 2, 0)))
    carry_ref[...] = u[tm - 8:, :]
    conv = cw_ref[0:1, :] * u2 + cw_ref[1:2, :] * u1 + cw_ref[2:3, :] * u
    a_b = _dot(h, win_ref[:, 0:CONV_W])
    ya_ref[...] = (a_b * conv).astype(BF16)

    cst = cs_ref[...]
    cs = jnp.concatenate([cst, jnp.zeros((LANES - cst.shape[0], tm), F32)], axis=0).T
    lane = lax.broadcasted_iota(jnp.int32, cs.shape, 1)
    half = ROPE // 2
    c = jnp.where(lane < NOPE, 1.0, jnp.where(lane < NOPE + ROPE, pltpu.roll(cs, NOPE, 1), 0.0))
    sin = pltpu.roll(cs, NOPE - ROPE, 1)
    sa = jnp.where((lane >= NOPE + half) & (lane < NOPE + ROPE), sin, 0.0)
    sbn = jnp.where((lane >= NOPE) & (lane < NOPE + half), -sin, 0.0)

    zb = _dot(h, win_ref[:, 3 * CONV_W:])
    qn = _rms(zb[:, :Q_RANK], qg_ref[...]).astype(BF16)
    kvn = _rms(zb[:, Q_RANK:Q_RANK + KV_RANK], kvg_ref[...]).astype(BF16)
    k_pe = _rope(pltpu.roll(zb[:, Q_RANK + KV_RANK:], NOPE, 1), c, sa, sbn)
    qp = _dot(qn, wqb_ref[...])
    qr = _dot(qn, wqr_ref[...])
    cq = c * q_scale
    sq = jnp.where((lane >= NOPE) & (lane < NOPE + ROPE), sin, 0.0) * q_scale
    for hd in range(N_HEADS):
        sl = slice(hd * HEAD_PAD, (hd + 1) * HEAD_PAD)
        q_ref[:, sl] = (qp[:, sl] * cq + qr[:, sl] * sq).astype(BF16)
    kp = _dot(kvn, wkb_ref[...])
    for hd in range(N_HEADS):
        sl = slice(hd * HEAD_PAD, (hd + 1) * HEAD_PAD)
        k_ref[:, sl] = (kp[:, sl] + k_pe).astype(BF16)
    v_ref[...] = (_dot(kvn, wvb_ref[...]) + vones_ref[...]).astype(BF16)


def _even_in(x, j, layer, w, cs, *, seq, tm):
    t = x.shape[0]
    row = lambda width: pl.BlockSpec((tm, width), lambda i: (i, 0))
    q_scale = float((NOPE + ROPE) ** -0.5 * np.log2(np.e))
    wide = jax.ShapeDtypeStruct((t, N_HEADS * HEAD_PAD), BF16)
    return pl.pallas_call(
        functools.partial(_even_in_kernel, tiles_per_seq=seq // tm, q_scale=q_scale),
        grid=(t // tm,),
        in_specs=[row(D_MODEL), _layer_spec(w["mix_g"], layer), _layer_spec(w["ev_in"], j),
                  _layer_spec(w["ev_conv"], j), _layer_spec(w["ev_qg"], j), _layer_spec(w["ev_qb"], j),
                  _layer_spec(w["ev_qr"], j),
                  _layer_spec(w["ev_kvg"], j), _layer_spec(w["ev_kb"], j), _layer_spec(w["ev_vb"], j),
                  _const_spec(w["v_ones"].shape), pl.BlockSpec((2 * ROPE, tm), lambda i: (0, i))],
        out_specs=[row(CONV_W), row(N_HEADS * HEAD_PAD), row(N_HEADS * HEAD_PAD), row(N_HEADS * HEAD_PAD)],
        out_shape=[jax.ShapeDtypeStruct((t, CONV_W), BF16), wide, wide, wide],
        scratch_shapes=[pltpu.VMEM((8, CONV_W), F32)],
        compiler_params=_params("arbitrary"),
        name="even_in",
    )(x, w["mix_g"], w["ev_in"], w["ev_conv"], w["ev_qg"], w["ev_qb"], w["ev_qr"], w["ev_kvg"], w["ev_kb"], w["ev_vb"],
      w["v_ones"], cs)


def _attn_kernel(q_ref, k_ref, v_ref, o_ref, *, tk, kv_step, n_q):
    tq = q_ref.shape[0]
    n_sub = tq // tk
    qi = pl.program_id(2)

    def update(state, hd, r0, rows, start, width, diag):
        lanes = slice(hd * HEAD_PAD, (hd + 1) * HEAD_PAD)
        s = _dot_nt(q_ref[r0:r0 + rows, lanes], k_ref[start:start + width, lanes])
        if diag:
            row = lax.broadcasted_iota(jnp.int32, (rows, width), 0)
            col = lax.broadcasted_iota(jnp.int32, (rows, width), 1)
            s = jnp.where(col <= row + (width - rows), s, -jnp.inf)
        m_blk = jnp.max(s, axis=-1, keepdims=True)
        if state is None:
            m_new = m_blk
        else:
            m, acc = state
            m_new = jnp.maximum(m, m_blk)
        pv = _dot(jnp.exp2(s - m_new).astype(BF16), v_ref[start:start + width, lanes])
        if state is not None:
            pv = jnp.exp2(m - m_new) * acc + pv
        return m_new, pv

    def tile_body(c):
        states = [None, None]
        for start in range(0, c * tq, kv_step):
            states = [update(states[hd], hd, 0, tq, start, kv_step, False) for hd in range(2)]
        lane = lax.broadcasted_iota(jnp.int32, (tk, HEAD_PAD), 1)
        for a in range(n_sub):
            rows = slice(a * tk, (a + 1) * tk)
            accs = []
            for hd in range(2):
                st = None if states[hd] is None else (states[hd][0][rows], states[hd][1][rows])
                accs.append(update(st, hd, a * tk, tk, c * tq, (a + 1) * tk, True)[1])
            o0 = accs[0] / accs[0][:, V_DIM:V_DIM + 1]
            o1 = accs[1] / accs[1][:, 0:1]
            o_ref[rows, :] = jnp.where(lane < V_DIM, o0, o1).astype(BF16)

    for c in range(n_q):
        pl.when(qi == c)(functools.partial(tile_body, c))


def _attention(q, k, v, *, batch, seq, tq, tk):
    t = q.shape[0]
    nq = seq // tq
    return pl.pallas_call(
        functools.partial(_attn_kernel, tk=tk, kv_step=min(ATTN_KV_STEP, tq), n_q=nq),
        grid=(batch, N_HEADS // 2, nq),
        in_specs=[pl.BlockSpec((tq, 2 * HEAD_PAD), lambda b, p, i: (b * nq + i, p)),
                  pl.BlockSpec((seq, 2 * HEAD_PAD), lambda b, p, i: (b, p)),
                  pl.BlockSpec((seq, 2 * HEAD_PAD), lambda b, p, i: (b, p))],
        out_specs=pl.BlockSpec((tq, 2 * V_DIM), lambda b, p, i: (b * nq + i, p)),
        out_shape=jax.ShapeDtypeStruct((t, N_HEADS * V_DIM), BF16),
        compiler_params=_params("arbitrary", "arbitrary", "arbitrary"),
        name="attention",
    )(q, k, v)


def _chunk_cumsum(x):
    pos = lax.broadcasted_iota(jnp.int32, x.shape, 0) & (GLA_CHUNK - 1)
    d = 1
    while d < GLA_CHUNK:
        x = x + jnp.where(pos >= d, pltpu.roll(x, d, 0), 0.0)
        d *= 2
    return x


def _odd_in_kernel(x_ref, g_ref, win_ref, wl_ref, wg2_ref, bg2_ref,
                   qe_ref, ke_ref, klt_lo_ref, klt_hi_ref, v_ref, gate_ref, dect_ref, *, q_scale, ts):
    tm = x_ref.shape[0]
    nc = tm // GLA_CHUNK
    q0, k0, v0, g0 = 0, GLA_DQK, 2 * GLA_DQK, 2 * GLA_DQK + GLA_DVS
    h = _rms(x_ref[...], g_ref[...]).astype(BF16)
    g_low = _dot(h, wl_ref[...]).astype(BF16)
    v_ref[...] = _dot(h, win_ref[:, v0:g0]).astype(BF16)
    gate_ref[...] = _dot(h, win_ref[:, g0:])
    gate_in = _dot(g_low, wg2_ref[...]) + bg2_ref[...]
    log_a = (jnp.minimum(gate_in, 0.0) - jnp.log1p(jnp.exp(-jnp.abs(gate_in)))) * (1.0 / GATE_TAU)
    bc = _chunk_cumsum(log_a)
    q = _dot(h, win_ref[:, q0:k0])
    qe_ref[...] = (q * q_scale * jnp.exp(bc)).astype(BF16)
    k = _dot(h, win_ref[:, k0:v0])
    ke_ref[...] = (k * jnp.exp(-bc)).astype(BF16)
    kl_parts = []
    dec_parts = []
    for ci in range(nc):
        rows = slice(ci * GLA_CHUNK, (ci + 1) * GLA_CHUNK)
        b_last = bc[(ci + 1) * GLA_CHUNK - 1:(ci + 1) * GLA_CHUNK, :]
        dec_parts.append(jnp.exp(b_last))
        kl_parts.append(k[rows, :] * jnp.exp(b_last - bc[rows, :]))
    klt = jnp.concatenate(kl_parts, axis=0).T
    odd_chunk = (lax.broadcasted_iota(jnp.int32, klt.shape, 1) & GLA_CHUNK) != 0
    klt_lo_ref[...] = jnp.where(odd_chunk, 0.0, klt).astype(BF16)
    klt_hi_ref[...] = jnp.where(odd_chunk, klt, 0.0).astype(BF16)
    per = ts // GLA_CHUNK
    for t0 in range(nc // per):
        dec_pad = jnp.concatenate(dec_parts[t0 * per:(t0 + 1) * per] + [jnp.zeros((DEC_LANES - per, GLA_DQK), F32)],
                                  axis=0)
        dect_ref[t0 * GLA_DQK:(t0 + 1) * GLA_DQK, :] = dec_pad.T


def _odd_in(x, j, layer, w, *, tm, ts):
    t = x.shape[0]
    row = lambda width: pl.BlockSpec((tm, width), lambda i: (i, 0))
    col = pl.BlockSpec((GLA_DQK, tm), lambda i: (0, i))
    return pl.pallas_call(
        functools.partial(_odd_in_kernel, q_scale=float(GLA_DK ** -0.5), ts=ts),
        grid=(t // tm,),
        in_specs=[row(D_MODEL), _layer_spec(w["mix_g"], layer), _layer_spec(w["od_in"], j),
                  _layer_spec(w["od_l"], j), _layer_spec(w["od_g2"], j), _layer_spec(w["od_b2"], j)],
        out_specs=[row(GLA_DQK), row(GLA_DQK), col, col, row(GLA_DVS), row(GLA_DVS),
                   pl.BlockSpec((tm // ts * GLA_DQK, DEC_LANES), lambda i: (i, 0))],
        out_shape=[jax.ShapeDtypeStruct((t, GLA_DQK), BF16), jax.ShapeDtypeStruct((t, GLA_DQK), BF16),
                   jax.ShapeDtypeStruct((GLA_DQK, t), BF16), jax.ShapeDtypeStruct((GLA_DQK, t), BF16),
                   jax.ShapeDtypeStruct((t, GLA_DVS), BF16), jax.ShapeDtypeStruct((t, GLA_DVS), F32),
                   jax.ShapeDtypeStruct((t // ts * GLA_DQK, DEC_LANES), F32)],
        compiler_params=_params("arbitrary"),
        name="odd_in",
    )(x, w["mix_g"], w["od_in"], w["od_l"], w["od_g2"], w["od_b2"])


def _gla_kernel(qe_ref, ke_ref, lo0_ref, lo1_ref, hi0_ref, hi1_ref, v_ref, gate_ref, dect_ref, og_ref,
                o_ref, state_ref):
    nb, ts, _ = qe_ref.shape
    pair = 2 * GLA_CHUNK
    klt_refs = ((lo0_ref, hi0_ref), (lo1_ref, hi1_ref))

    @pl.when(pl.program_id(1) == 0)
    def _():
        state_ref[...] = jnp.zeros_like(state_ref)

    r = lax.broadcasted_iota(jnp.int32, (pair, pair), 0)
    c = lax.broadcasted_iota(jnp.int32, (pair, pair), 1)
    intra = (c <= r) & ((r < GLA_CHUNK) | (c >= GLA_CHUNK))
    og = og_ref[...]
    n_pairs = ts // pair
    res_all = {}
    for cp in range(n_pairs):
        rows = slice(cp * pair, (cp + 1) * pair)
        for bi in range(nb):
            for hd in range(GLA_HEADS):
                dk = slice(hd * GLA_DK, (hd + 1) * GLA_DK)
                dv = slice(hd * GLA_DV, (hd + 1) * GLA_DV)
                a = jnp.where(intra, _dot_nt(qe_ref[bi, rows, dk], ke_ref[bi, rows, dk]), 0.0).astype(BF16)
                lhs = jnp.concatenate([a, klt_refs[bi][0][dk, rows], klt_refs[bi][1][dk, rows]], axis=0)
                res_all[cp, bi, hd] = _dot(lhs, v_ref[bi, rows, dv])
    for bi in range(nb):
        states = [state_ref[bi, hd] for hd in range(GLA_HEADS)]
        for cp in range(n_pairs):
            rows = slice(cp * pair, (cp + 1) * pair)
            heads = []
            for hd in range(GLA_HEADS):
                dk = slice(hd * GLA_DK, (hd + 1) * GLA_DK)
                res = res_all[cp, bi, hd]
                outs = []
                for ci in range(2):
                    q_c = qe_ref[bi, cp * pair + ci * GLA_CHUNK:cp * pair + (ci + 1) * GLA_CHUNK, dk]
                    outs.append(_dot(q_c, states[hd].astype(BF16)))
                    dcol = dect_ref[bi, dk, 2 * cp + ci:2 * cp + ci + 1]
                    states[hd] = states[hd] * dcol + res[pair + ci * GLA_DK:pair + (ci + 1) * GLA_DK]
                heads.append(_rms(res[:pair] + jnp.concatenate(outs, axis=0), og))
            gt = gate_ref[bi, rows, :]
            o_ref[bi, rows, :] = (jnp.concatenate(heads, axis=1) * (gt * jax.nn.sigmoid(gt))).astype(BF16)
        for hd in range(GLA_HEADS):
            state_ref[bi, hd] = states[hd]


def _gla(qe, ke, klt_lo, klt_hi, v, gate, dect, j, w, *, batch, seq, ts):
    nb = 2
    nt = seq // ts
    r3 = lambda a: a.reshape(batch, seq, a.shape[-1])
    blk = lambda width: pl.BlockSpec((nb, ts, width), lambda p, t: (p, t, 0))
    klt = lambda bi: pl.BlockSpec((GLA_DQK, ts), lambda p, t: (0, (nb * p + bi) * nt + t))
    out = pl.pallas_call(
        _gla_kernel,
        grid=(batch // nb, nt),
        in_specs=[blk(GLA_DQK), blk(GLA_DQK), klt(0), klt(1), klt(0), klt(1), blk(GLA_DVS), blk(GLA_DVS),
                  pl.BlockSpec((nb, GLA_DQK, DEC_LANES), lambda p, t: (p, t, 0)), _layer_spec(w["od_og"], j)],
        out_specs=blk(GLA_DVS),
        out_shape=jax.ShapeDtypeStruct((batch, seq, GLA_DVS), BF16),
        scratch_shapes=[pltpu.VMEM((nb, GLA_HEADS, GLA_DK, GLA_DV), F32)],
        compiler_params=_params("arbitrary", "arbitrary"),
        name="gla",
    )(r3(qe), r3(ke), klt_lo, klt_lo, klt_hi, klt_hi, r3(v), r3(gate),
      dect.reshape(batch, nt * GLA_DQK, DEC_LANES), w["od_og"])
    return out.reshape(batch * seq, GLA_DVS)


def _mix_mlp_kernel(x_ref, ma_ref, mb_ref, wo_ref, g_ref, w1_ref, w2_ref, fg_ref, o_ref, *, final_norm):
    y = x_ref[...] + _dot(jnp.concatenate([ma_ref[...], mb_ref[...]], axis=1), wo_ref[...])
    h = _rms(y, g_ref[...]).astype(BF16)
    acc = y
    for c0 in range(0, D_FF, FF_CHUNK):
        u = _dot(h, w1_ref[:, c0:c0 + FF_CHUNK])
        u = jnp.square(jnp.maximum(u, 0.0)).astype(BF16)
        acc = acc + _dot(u, w2_ref[c0:c0 + FF_CHUNK, :])
    if final_norm:
        acc = _rms(acc, fg_ref[...])
    o_ref[...] = acc


def _mix_mlp(x, mix_a, mix_b, wo, j, layer, w, *, tm, final_norm):
    t = x.shape[0]
    half = D_MODEL // 2
    row = lambda width: pl.BlockSpec((tm, width), lambda i: (i, 0))
    b_spec = row(half) if mix_b is not mix_a else pl.BlockSpec((tm, half), lambda i: (i, 1))
    return pl.pallas_call(
        functools.partial(_mix_mlp_kernel, final_norm=final_norm),
        grid=(t // tm,),
        in_specs=[row(D_MODEL), row(half), b_spec, _layer_spec(wo, j), _layer_spec(w["mlp_g"], layer),
                  _layer_spec(w["mlp_w1"], layer), _layer_spec(w["mlp_w2"], layer),
                  _const_spec(w["final_g"].shape)],
        out_specs=row(D_MODEL),
        out_shape=jax.ShapeDtypeStruct((t, D_MODEL), F32),
        compiler_params=_params("arbitrary"),
        name="mix_mlp",
    )(x, mix_a, mix_b, wo, w["mlp_g"], w["mlp_w1"], w["mlp_w2"], w["final_g"])


def _rope_table(positions):
    inv_freq = 1.0 / (ROPE_THETA ** (jnp.arange(0, ROPE, 2, dtype=F32) / ROPE))
    ang = inv_freq.reshape(-1, 1) * positions.astype(F32).reshape(1, -1)
    cos = jnp.cos(ang)
    sin = jnp.sin(ang)
    return jnp.concatenate([cos, cos, sin, sin], axis=0)


def _prepare_weights(mix_norm_g, mlp_norm_g, final_norm_g, ev_w_in, ev_conv_w, ev_q_norm_g, ev_w_qb,
                     ev_kv_norm_g, ev_w_kvb, ev_w_out, od_w_in, od_w_gate2, od_b_gate2, od_o_norm_g, od_w_out,
                     mlp_w1, mlp_w2):
    cast = lambda a: a.astype(BF16)
    vec = lambda a: a.reshape(a.shape[0], 1, a.shape[1])
    n_even = ev_w_in.shape[0]
    pad_last = lambda a, n: jnp.pad(a, [(0, 0)] * (a.ndim - 1) + [(0, n)])
    head_pad = HEAD_PAD - NOPE - ROPE
    wqb = pad_last(ev_w_qb.reshape(n_even, Q_RANK, N_HEADS, NOPE + ROPE), head_pad)
    half = ROPE // 2
    wqr = jnp.concatenate([jnp.zeros_like(wqb[..., :NOPE]), -wqb[..., NOPE + half:NOPE + ROPE],
                           wqb[..., NOPE:NOPE + half], jnp.zeros_like(wqb[..., NOPE + ROPE:])], axis=-1)
    wkvb = ev_w_kvb.reshape(n_even, KV_RANK, N_HEADS, NOPE + V_DIM)
    wkb = pad_last(wkvb[..., :NOPE], HEAD_PAD - NOPE)
    wv = wkvb[..., NOPE:]
    odd = (jnp.arange(N_HEADS) % 2 == 1)[None, None, :, None]
    wvb = jnp.where(odd, jnp.pad(wv, ((0, 0), (0, 0), (0, 0), (V_DIM, 0))), pad_last(wv, V_DIM))
    flat = lambda a: a.reshape(a.shape[0], a.shape[1], N_HEADS * HEAD_PAD)
    lane = np.arange(N_HEADS * HEAD_PAD) % (2 * HEAD_PAD)
    v_ones = jnp.asarray(((lane == V_DIM) | (lane == HEAD_PAD)).astype(np.float32)).reshape(1, -1)
    return {
        "mix_g": vec(mix_norm_g), "mlp_g": vec(mlp_norm_g), "final_g": final_norm_g.reshape(1, D_MODEL),
        "ev_in": pad_last(cast(ev_w_in), EVEN_IN_PAD - ev_w_in.shape[-1]), "ev_conv": ev_conv_w,
        "ev_qg": vec(ev_q_norm_g), "ev_qb": cast(flat(wqb)), "ev_qr": cast(flat(wqr)), "ev_kvg": vec(ev_kv_norm_g),
        "ev_kb": cast(flat(wkb)), "ev_vb": cast(flat(wvb)), "v_ones": v_ones, "ev_out": cast(ev_w_out),
        "od_in": cast(od_w_in[..., :ODD_IN_MAIN]),
        "od_l": pad_last(cast(od_w_in[..., ODD_IN_MAIN:]), GATE_PAD - GATE_RANK),
        "od_g2": cast(jnp.pad(od_w_gate2, ((0, 0), (0, GATE_PAD - GATE_RANK), (0, 0)))),
        "od_b2": vec(od_b_gate2), "od_og": vec(od_o_norm_g), "od_out": cast(od_w_out),
        "mlp_w1": cast(mlp_w1), "mlp_w2": cast(mlp_w2),
    }


def kernel(x, positions, mix_norm_g, mlp_norm_g, final_norm_g, ev_w_in, ev_conv_w, ev_q_norm_g, ev_w_qb,
           ev_kv_norm_g, ev_w_kvb, ev_w_out, od_w_in, od_w_gate2, od_b_gate2, od_o_norm_g, od_w_out,
           mlp_w1, mlp_w2):
    batch, seq, _ = x.shape
    depth = mix_norm_g.shape[0]
    tm = min(ROW_TILE, seq)
    ts = min(GLA_TILE, tm)
    tq = min(ATTN_Q_TILE, seq)
    tk = min(ATTN_SUB_TILE, tq)
    w = _prepare_weights(mix_norm_g, mlp_norm_g, final_norm_g, ev_w_in, ev_conv_w, ev_q_norm_g, ev_w_qb,
                         ev_kv_norm_g, ev_w_kvb, ev_w_out, od_w_in, od_w_gate2, od_b_gate2, od_o_norm_g,
                         od_w_out, mlp_w1, mlp_w2)
    cs = _rope_table(positions)
    xf = x.reshape(batch * seq, D_MODEL)
    for layer in range(depth):
        j = layer // 2
        if layer % 2 == 0:
            ya, q, k, v = _even_in(xf, j, layer, w, cs, seq=seq, tm=tm)
            yb = _attention(q, k, v, batch=batch, seq=seq, tq=tq, tk=tk)
            mix_a, mix_b, wo = ya, yb, w["ev_out"]
        else:
            qe, ke, klt_lo, klt_hi, v, gate, dect = _odd_in(xf, j, layer, w, tm=tm, ts=ts)
            mix_a = mix_b = _gla(qe, ke, klt_lo, klt_hi, v, gate, dect, j, w, batch=batch, seq=seq, ts=ts)
            wo = w["od_out"]
        xf = _mix_mlp(xf, mix_a, mix_b, wo, j, layer, w, tm=min(MLP_ROW_TILE, seq),
                      final_norm=(layer == depth - 1))
    return xf.reshape(batch, seq, D_MODEL)
```

```python
import functools

import jax
import jax.numpy as jnp
import numpy as np
from jax import lax
from jax.experimental import pallas as pl
from jax.experimental.pallas import tpu as pltpu

D_MODEL = 1024
CONV_W = 512
N_HEADS = 8
NOPE = 64
ROPE = 32
V_DIM = 64
Q_RANK = 384
KV_RANK = 256
ROPE_THETA = 10000.0
LANES = 128
HEAD_PAD = LANES
EVEN_IN_PAD = 3 * CONV_W + Q_RANK + KV_RANK + LANES
GLA_HEADS = 4
GLA_DK = 128
GLA_DV = 256
GLA_DQK = GLA_HEADS * GLA_DK
GLA_DVS = GLA_HEADS * GLA_DV
GATE_RANK = 16
GATE_PAD = LANES
ODD_IN_PAD = 2 * GLA_DQK + 2 * GLA_DVS + GATE_PAD
GATE_TAU = 16.0
GLA_CHUNK = 64
DEC_LANES = LANES
D_FF = 4 * D_MODEL
EPS = 1e-6

BF16 = jnp.bfloat16
F32 = jnp.float32

VMEM_LIMIT_BYTES = 56 * 1024 * 1024
ROW_TILE = 1024
GLA_TILE = 512
MLP_ROW_TILE = 1024
ATTN_HEADS = 4
ATTN_Q_TILE = 2048
ATTN_KV_STEP = 1024
ATTN_SUB_TILE = 512
FF_CHUNK = 512


def _rms(x, g):
    ms = jnp.mean(x * x, axis=-1, keepdims=True)
    return x * lax.rsqrt(ms + EPS) * g


def _dot(a, b):
    return jnp.dot(a, b, preferred_element_type=F32)


def _dot_nt(a, b):
    return lax.dot_general(a, b, (((1,), (1,)), ((), ())), preferred_element_type=F32)


def _const_spec(shape):
    nd = len(shape)
    return pl.BlockSpec(shape, lambda *_: (0,) * nd, pipeline_mode=pl.Buffered(1))


def _layer_spec(arr, j):
    shape = arr.shape[1:]
    nd = len(shape)
    return pl.BlockSpec((None,) + shape, lambda *_: (j,) + (0,) * nd, pipeline_mode=pl.Buffered(1))


def _params(*sem):
    return pltpu.CompilerParams(dimension_semantics=sem, vmem_limit_bytes=VMEM_LIMIT_BYTES)


def _rope(p, c, sa, sbn):
    return p * c + pltpu.roll(p, 16, 1) * sa + pltpu.roll(p, 112, 1) * sbn


def _even_in_kernel(x_ref, g_ref, win_ref, cw_ref, qg_ref, wqb_ref, wqr_ref, kvg_ref, wkb_ref, wvb_ref, vones_ref,
                    cs_ref, ya_ref, q_ref, k_ref, v_ref, carry_ref, *, tiles_per_seq, q_scale):
    tm = x_ref.shape[0]
    i = pl.program_id(0)
    h = _rms(x_ref[...], g_ref[...]).astype(BF16)

    a_c = _dot(h, win_ref[:, CONV_W:2 * CONV_W])
    a_v = _dot(h, win_ref[:, 2 * CONV_W:3 * CONV_W])
    u = a_c * a_v

    @pl.when(i % tiles_per_seq == 0)
    def _():
        carry_ref[...] = jnp.zeros_like(carry_ref)

    prev1 = carry_ref[7:8, :]
    prev2 = carry_ref[6:7, :]
    row = lax.broadcasted_iota(jnp.int32, u.shape, 0)
    u1 = jnp.where(row == 0, prev1, pltpu.roll(u, 1, 0))
    u2 = jnp.where(row == 0, prev2, jnp.where(row == 1, prev1, pltpu.roll(u, 2, 0)))
    carry_ref[...] = u[tm - 8:, :]
    conv = cw_ref[0:1, :] * u2 + cw_ref[1:2, :] * u1 + cw_ref[2:3, :] * u
    a_b = _dot(h, win_ref[:, 0:CONV_W])
    ya_ref[...] = (a_b * conv).astype(BF16)

    cst = cs_ref[...]
    cs = jnp.concatenate([cst, jnp.zeros((LANES - cst.shape[0], tm), F32)], axis=0).T
    lane = lax.broadcasted_iota(jnp.int32, cs.shape, 1)
    half = ROPE // 2
    c = jnp.where(lane < NOPE, 1.0, jnp.where(lane < NOPE + ROPE, pltpu.roll(cs, NOPE, 1), 0.0))
    sin = pltpu.roll(cs, NOPE - ROPE, 1)
    sa = jnp.where((lane >= NOPE + half) & (lane < NOPE + ROPE), sin, 0.0)
    sbn = jnp.where((lane >= NOPE) & (lane < NOPE + half), -sin, 0.0)

    zb = _dot(h, win_ref[:, 3 * CONV_W:])
    qn = _rms(zb[:, :Q_RANK], qg_ref[...]).astype(BF16)
    kvn = _rms(zb[:, Q_RANK:Q_RANK + KV_RANK], kvg_ref[...]).astype(BF16)
    k_pe = _rope(pltpu.roll(zb[:, Q_RANK + KV_RANK:], NOPE, 1), c, sa, sbn)
    qp = _dot(qn, wqb_ref[...])
    qr = _dot(qn, wqr_ref[...])
    cq = c * q_scale
    sq = jnp.where((lane >= NOPE) & (lane < NOPE + ROPE), sin, 0.0) * q_scale
    for hd in range(N_HEADS):
        sl = slice(hd * HEAD_PAD, (hd + 1) * HEAD_PAD)
        q_ref[:, sl] = (qp[:, sl] * cq + qr[:, sl] * sq).astype(BF16)
    kp = _dot(kvn, wkb_ref[...])
    for hd in range(N_HEADS):
        sl = slice(hd * HEAD_PAD, (hd + 1) * HEAD_PAD)
        k_ref[:, sl] = (kp[:, sl] + k_pe).astype(BF16)
    v_ref[...] = (_dot(kvn, wvb_ref[...]) + vones_ref[...]).astype(BF16)


def _even_in(x, j, layer, w, cs, *, seq, tm):
    t = x.shape[0]
    row = lambda width: pl.BlockSpec((tm, width), lambda i: (i, 0))
    q_scale = float((NOPE + ROPE) ** -0.5 * np.log2(np.e))
    wide = jax.ShapeDtypeStruct((t, N_HEADS * HEAD_PAD), BF16)
    return pl.pallas_call(
        functools.partial(_even_in_kernel, tiles_per_seq=seq // tm, q_scale=q_scale),
        grid=(t // tm,),
        in_specs=[row(D_MODEL), _layer_spec(w["mix_g"], layer), _layer_spec(w["ev_in"], j),
                  _layer_spec(w["ev_conv"], j), _layer_spec(w["ev_qg"], j), _layer_spec(w["ev_qb"], j),
                  _layer_spec(w["ev_qr"], j),
                  _layer_spec(w["ev_kvg"], j), _layer_spec(w["ev_kb"], j), _layer_spec(w["ev_vb"], j),
                  _const_spec(w["v_ones"].shape), pl.BlockSpec((2 * ROPE, tm), lambda i: (0, i))],
        out_specs=[row(CONV_W), row(N_HEADS * HEAD_PAD), row(N_HEADS * HEAD_PAD), row(N_HEADS * HEAD_PAD)],
        out_shape=[jax.ShapeDtypeStruct((t, CONV_W), BF16), wide, wide, wide],
        scratch_shapes=[pltpu.VMEM((8, CONV_W), F32)],
        compiler_params=_params("arbitrary"),
        name="even_in",
    )(x, w["mix_g"], w["ev_in"], w["ev_conv"], w["ev_qg"], w["ev_qb"], w["ev_qr"], w["ev_kvg"], w["ev_kb"], w["ev_vb"],
      w["v_ones"], cs)


def _attn_kernel(q_ref, k_ref, v_ref, o_ref, *, tk, kv_step, n_q):
    tq = q_ref.shape[0]
    n_sub = tq // tk
    n_heads = q_ref.shape[1] // HEAD_PAD
    qi = pl.program_id(2)

    def update(state, hd, r0, rows, start, width, diag):
        lanes = slice(hd * HEAD_PAD, (hd + 1) * HEAD_PAD)
        s = _dot_nt(q_ref[r0:r0 + rows, lanes], k_ref[start:start + width, lanes])
        if diag:
            row = lax.broadcasted_iota(jnp.int32, (rows, width), 0)
            col = lax.broadcasted_iota(jnp.int32, (rows, width), 1)
            s = jnp.where(col <= row + (width - rows), s, -jnp.inf)
        m_blk = jnp.max(s, axis=-1, keepdims=True)
        if state is None:
            m_new = m_blk
        else:
            m, acc = state
            m_new = jnp.maximum(m, m_blk)
        pv = _dot(jnp.exp2(s - m_new).astype(BF16), v_ref[start:start + width, lanes])
        if state is not None:
            pv = jnp.exp2(m - m_new) * acc + pv
        return m_new, pv

    def tile_body(c):
        states = [None] * n_heads
        for start in range(0, c * tq, kv_step):
            states = [update(states[hd], hd, 0, tq, start, kv_step, False) for hd in range(n_heads)]
        lane = lax.broadcasted_iota(jnp.int32, (tk, HEAD_PAD), 1)
        for a in range(n_sub):
            rows = slice(a * tk, (a + 1) * tk)
            accs = []
            for hd in range(n_heads):
                st = None if states[hd] is None else (states[hd][0][rows], states[hd][1][rows])
                accs.append(update(st, hd, a * tk, tk, c * tq, (a + 1) * tk, True)[1])
            for pr in range(n_heads // 2):
                o0 = accs[2 * pr] / accs[2 * pr][:, V_DIM:V_DIM + 1]
                o1 = accs[2 * pr + 1] / accs[2 * pr + 1][:, 0:1]
                o_ref[rows, pr * HEAD_PAD:(pr + 1) * HEAD_PAD] = jnp.where(lane < V_DIM, o0, o1).astype(BF16)

    for c in range(n_q):
        pl.when(qi == c)(functools.partial(tile_body, c))


def _attention(q, k, v, *, batch, seq, tq, tk):
    t = q.shape[0]
    nq = seq // tq
    return pl.pallas_call(
        functools.partial(_attn_kernel, tk=tk, kv_step=min(ATTN_KV_STEP, tq), n_q=nq),
        grid=(batch, N_HEADS // ATTN_HEADS, nq),
        in_specs=[pl.BlockSpec((tq, ATTN_HEADS * HEAD_PAD), lambda b, p, i: (b * nq + i, p)),
                  pl.BlockSpec((seq, ATTN_HEADS * HEAD_PAD), lambda b, p, i: (b, p)),
                  pl.BlockSpec((seq, ATTN_HEADS * HEAD_PAD), lambda b, p, i: (b, p))],
        out_specs=pl.BlockSpec((tq, ATTN_HEADS * V_DIM), lambda b, p, i: (b * nq + i, p)),
        out_shape=jax.ShapeDtypeStruct((t, N_HEADS * V_DIM), BF16),
        compiler_params=_params("arbitrary", "arbitrary", "arbitrary"),
        name="attention",
    )(q, k, v)


def _chunk_cumsum(x):
    pos = lax.broadcasted_iota(jnp.int32, x.shape, 0) & (GLA_CHUNK - 1)
    d = 1
    while d < GLA_CHUNK:
        x = x + jnp.where(pos >= d, pltpu.roll(x, d, 0), 0.0)
        d *= 2
    return x


def _odd_in_kernel(x_ref, g_ref, win_ref, wg2_ref, bg2_ref,
                   qe_ref, ke_ref, klt_lo_ref, klt_hi_ref, v_ref, gate_ref, dect_ref, *, q_scale, ts):
    tm = x_ref.shape[0]
    nc = tm // GLA_CHUNK
    q0, k0, v0, g0, l0 = 0, GLA_DQK, 2 * GLA_DQK, 2 * GLA_DQK + GLA_DVS, 2 * GLA_DQK + 2 * GLA_DVS
    h = _rms(x_ref[...], g_ref[...]).astype(BF16)
    g_low = _dot(h, win_ref[:, l0:]).astype(BF16)
    gate_in = _dot(g_low, wg2_ref[...]) + bg2_ref[...]
    log_a = (jnp.minimum(gate_in, 0.0) - jnp.log1p(jnp.exp(-jnp.abs(gate_in)))) * (1.0 / GATE_TAU)
    bc = _chunk_cumsum(log_a)
    q = _dot(h, win_ref[:, q0:k0])
    qe_ref[...] = (q * q_scale * jnp.exp(bc)).astype(BF16)
    k = _dot(h, win_ref[:, k0:v0])
    ke_ref[...] = (k * jnp.exp(-bc)).astype(BF16)
    kl_parts = []
    dec_parts = []
    for ci in range(nc):
        rows = slice(ci * GLA_CHUNK, (ci + 1) * GLA_CHUNK)
        b_last = bc[(ci + 1) * GLA_CHUNK - 1:(ci + 1) * GLA_CHUNK, :]
        dec_parts.append(jnp.exp(b_last))
        kl_parts.append(k[rows, :] * jnp.exp(b_last - bc[rows, :]))
    klt = jnp.concatenate(kl_parts, axis=0).T
    odd_chunk = (lax.broadcasted_iota(jnp.int32, klt.shape, 1) & GLA_CHUNK) != 0
    klt_lo_ref[...] = jnp.where(odd_chunk, 0.0, klt).astype(BF16)
    klt_hi_ref[...] = jnp.where(odd_chunk, klt, 0.0).astype(BF16)
    per = ts // GLA_CHUNK
    for t0 in range(nc // per):
        dec_pad = jnp.concatenate(dec_parts[t0 * per:(t0 + 1) * per] + [jnp.zeros((DEC_LANES - per, GLA_DQK), F32)],
                                  axis=0)
        dect_ref[t0 * GLA_DQK:(t0 + 1) * GLA_DQK, :] = dec_pad.T
    v_ref[...] = _dot(h, win_ref[:, v0:g0]).astype(BF16)
    gate_ref[...] = _dot(h, win_ref[:, g0:l0])


def _odd_in(x, j, layer, w, *, tm, ts):
    t = x.shape[0]
    row = lambda width: pl.BlockSpec((tm, width), lambda i: (i, 0))
    col = pl.BlockSpec((GLA_DQK, tm), lambda i: (0, i))
    return pl.pallas_call(
        functools.partial(_odd_in_kernel, q_scale=float(GLA_DK ** -0.5), ts=ts),
        grid=(t // tm,),
        in_specs=[row(D_MODEL), _layer_spec(w["mix_g"], layer), _layer_spec(w["od_in"], j),
                  _layer_spec(w["od_g2"], j), _layer_spec(w["od_b2"], j)],
        out_specs=[row(GLA_DQK), row(GLA_DQK), col, col, row(GLA_DVS), row(GLA_DVS),
                   pl.BlockSpec((tm // ts * GLA_DQK, DEC_LANES), lambda i: (i, 0))],
        out_shape=[jax.ShapeDtypeStruct((t, GLA_DQK), BF16), jax.ShapeDtypeStruct((t, GLA_DQK), BF16),
                   jax.ShapeDtypeStruct((GLA_DQK, t), BF16), jax.ShapeDtypeStruct((GLA_DQK, t), BF16),
                   jax.ShapeDtypeStruct((t, GLA_DVS), BF16), jax.ShapeDtypeStruct((t, GLA_DVS), F32),
                   jax.ShapeDtypeStruct((t // ts * GLA_DQK, DEC_LANES), F32)],
        compiler_params=_params("arbitrary"),
        name="odd_in",
    )(x, w["mix_g"], w["od_in"], w["od_g2"], w["od_b2"])


def _gla_kernel(qe_ref, ke_ref, lo0_ref, lo1_ref, hi0_ref, hi1_ref, v_ref, gate_ref, dect_ref, og_ref,
                o_ref, state_ref):
    nb, ts, _ = qe_ref.shape
    pair = 2 * GLA_CHUNK
    klt_refs = ((lo0_ref, hi0_ref), (lo1_ref, hi1_ref))

    @pl.when(pl.program_id(1) == 0)
    def _():
        state_ref[...] = jnp.zeros_like(state_ref)

    r = lax.broadcasted_iota(jnp.int32, (pair, pair), 0)
    c = lax.broadcasted_iota(jnp.int32, (pair, pair), 1)
    intra = (c <= r) & ((r < GLA_CHUNK) | (c >= GLA_CHUNK))
    og = og_ref[...]
    for cp in range(ts // pair):
        rows = slice(cp * pair, (cp + 1) * pair)
        for bi in range(nb):
            heads = []
            for hd in range(GLA_HEADS):
                dk = slice(hd * GLA_DK, (hd + 1) * GLA_DK)
                dv = slice(hd * GLA_DV, (hd + 1) * GLA_DV)
                qe = qe_ref[bi, rows, dk]
                vv = v_ref[bi, rows, dv]
                a = jnp.where(intra, _dot_nt(qe, ke_ref[bi, rows, dk]), 0.0).astype(BF16)
                lhs = jnp.concatenate([a, klt_refs[bi][0][dk, rows], klt_refs[bi][1][dk, rows]], axis=0)
                res = _dot(lhs, vv)
                state = state_ref[bi, hd]
                outs = []
                for ci in range(2):
                    q_c = qe[ci * GLA_CHUNK:(ci + 1) * GLA_CHUNK]
                    outs.append(_dot(q_c, state.astype(BF16)))
                    dcol = dect_ref[bi, dk, 2 * cp + ci:2 * cp + ci + 1]
                    state = state * dcol + res[pair + ci * GLA_DK:pair + (ci + 1) * GLA_DK]
                state_ref[bi, hd] = state
                o = res[:pair] + jnp.concatenate(outs, axis=0)
                heads.append(_rms(o, og))
            gt = gate_ref[bi, rows, :]
            o_ref[bi, rows, :] = (jnp.concatenate(heads, axis=1) * (gt * jax.nn.sigmoid(gt))).astype(BF16)


def _gla(qe, ke, klt_lo, klt_hi, v, gate, dect, j, w, *, batch, seq, ts):
    nb = 2
    nt = seq // ts
    r3 = lambda a: a.reshape(batch, seq, a.shape[-1])
    blk = lambda width: pl.BlockSpec((nb, ts, width), lambda p, t: (p, t, 0))
    klt = lambda bi: pl.BlockSpec((GLA_DQK, ts), lambda p, t: (0, (nb * p + bi) * nt + t))
    out = pl.pallas_call(
        _gla_kernel,
        grid=(batch // nb, nt),
        in_specs=[blk(GLA_DQK), blk(GLA_DQK), klt(0), klt(1), klt(0), klt(1), blk(GLA_DVS), blk(GLA_DVS),
                  pl.BlockSpec((nb, GLA_DQK, DEC_LANES), lambda p, t: (p, t, 0)), _layer_spec(w["od_og"], j)],
        out_specs=blk(GLA_DVS),
        out_shape=jax.ShapeDtypeStruct((batch, seq, GLA_DVS), BF16),
        scratch_shapes=[pltpu.VMEM((nb, GLA_HEADS, GLA_DK, GLA_DV), F32)],
        compiler_params=_params("arbitrary", "arbitrary"),
        name="gla",
    )(r3(qe), r3(ke), klt_lo, klt_lo, klt_hi, klt_hi, r3(v), r3(gate),
      dect.reshape(batch, nt * GLA_DQK, DEC_LANES), w["od_og"])
    return out.reshape(batch * seq, GLA_DVS)


def _mix_mlp_kernel(x_ref, ma_ref, mb_ref, wo_ref, g_ref, w1_ref, w2_ref, fg_ref, o_ref, *, final_norm):
    y = x_ref[...] + _dot(jnp.concatenate([ma_ref[...], mb_ref[...]], axis=1), wo_ref[...])
    h = _rms(y, g_ref[...]).astype(BF16)
    acc = y
    for c0 in range(0, D_FF, FF_CHUNK):
        u = _dot(h, w1_ref[:, c0:c0 + FF_CHUNK])
        u = jnp.square(jnp.maximum(u, 0.0)).astype(BF16)
        acc = acc + _dot(u, w2_ref[c0:c0 + FF_CHUNK, :])
    if final_norm:
        acc = _rms(acc, fg_ref[...])
    o_ref[...] = acc


def _mix_mlp(x, mix_a, mix_b, wo, j, layer, w, *, tm, final_norm):
    t = x.shape[0]
    half = D_MODEL // 2
    row = lambda width: pl.BlockSpec((tm, width), lambda i: (i, 0))
    b_spec = row(half) if mix_b is not mix_a else pl.BlockSpec((tm, half), lambda i: (i, 1))
    return pl.pallas_call(
        functools.partial(_mix_mlp_kernel, final_norm=final_norm),
        grid=(t // tm,),
        in_specs=[row(D_MODEL), row(half), b_spec, _layer_spec(wo, j), _layer_spec(w["mlp_g"], layer),
                  _layer_spec(w["mlp_w1"], layer), _layer_spec(w["mlp_w2"], layer),
                  _const_spec(w["final_g"].shape)],
        out_specs=row(D_MODEL),
        out_shape=jax.ShapeDtypeStruct((t, D_MODEL), F32),
        compiler_params=_params("arbitrary"),
        name="mix_mlp",
    )(x, mix_a, mix_b, wo, w["mlp_g"], w["mlp_w1"], w["mlp_w2"], w["final_g"])


def _rope_table(positions):
    inv_freq = 1.0 / (ROPE_THETA ** (jnp.arange(0, ROPE, 2, dtype=F32) / ROPE))
    ang = inv_freq.reshape(-1, 1) * positions.astype(F32).reshape(1, -1)
    cos = jnp.cos(ang)
    sin = jnp.sin(ang)
    return jnp.concatenate([cos, cos, sin, sin], axis=0)


def _prepare_weights(mix_norm_g, mlp_norm_g, final_norm_g, ev_w_in, ev_conv_w, ev_q_norm_g, ev_w_qb,
                     ev_kv_norm_g, ev_w_kvb, ev_w_out, od_w_in, od_w_gate2, od_b_gate2, od_o_norm_g, od_w_out,
                     mlp_w1, mlp_w2):
    cast = lambda a: a.astype(BF16)
    vec = lambda a: a.reshape(a.shape[0], 1, a.shape[1])
    n_even = ev_w_in.shape[0]
    pad_last = lambda a, n: jnp.pad(a, [(0, 0)] * (a.ndim - 1) + [(0, n)])
    head_pad = HEAD_PAD - NOPE - ROPE
    wqb = pad_last(ev_w_qb.reshape(n_even, Q_RANK, N_HEADS, NOPE + ROPE), head_pad)
    half = ROPE // 2
    wqr = jnp.concatenate([jnp.zeros_like(wqb[..., :NOPE]), -wqb[..., NOPE + half:NOPE + ROPE],
                           wqb[..., NOPE:NOPE + half], jnp.zeros_like(wqb[..., NOPE + ROPE:])], axis=-1)
    wkvb = ev_w_kvb.reshape(n_even, KV_RANK, N_HEADS, NOPE + V_DIM)
    wkb = pad_last(wkvb[..., :NOPE], HEAD_PAD - NOPE)
    wv = wkvb[..., NOPE:]
    odd = (jnp.arange(N_HEADS) % 2 == 1)[None, None, :, None]
    wvb = jnp.where(odd, jnp.pad(wv, ((0, 0), (0, 0), (0, 0), (V_DIM, 0))), pad_last(wv, V_DIM))
    flat = lambda a: a.reshape(a.shape[0], a.shape[1], N_HEADS * HEAD_PAD)
    lane = np.arange(N_HEADS * HEAD_PAD) % (2 * HEAD_PAD)
    v_ones = jnp.asarray(((lane == V_DIM) | (lane == HEAD_PAD)).astype(np.float32)).reshape(1, -1)
    return {
        "mix_g": vec(mix_norm_g), "mlp_g": vec(mlp_norm_g), "final_g": final_norm_g.reshape(1, D_MODEL),
        "ev_in": cast(pad_last(ev_w_in, EVEN_IN_PAD - ev_w_in.shape[-1])), "ev_conv": ev_conv_w,
        "ev_qg": vec(ev_q_norm_g), "ev_qb": cast(flat(wqb)), "ev_qr": cast(flat(wqr)), "ev_kvg": vec(ev_kv_norm_g),
        "ev_kb": cast(flat(wkb)), "ev_vb": cast(flat(wvb)), "v_ones": v_ones, "ev_out": cast(ev_w_out),
        "od_in": cast(pad_last(od_w_in, ODD_IN_PAD - od_w_in.shape[-1])),
        "od_g2": cast(jnp.pad(od_w_gate2, ((0, 0), (0, GATE_PAD - GATE_RANK), (0, 0)))),
        "od_b2": vec(od_b_gate2), "od_og": vec(od_o_norm_g), "od_out": cast(od_w_out),
        "mlp_w1": cast(mlp_w1), "mlp_w2": cast(mlp_w2),
    }


def kernel(x, positions, mix_norm_g, mlp_norm_g, final_norm_g, ev_w_in, ev_conv_w, ev_q_norm_g, ev_w_qb,
           ev_kv_norm_g, ev_w_kvb, ev_w_out, od_w_in, od_w_gate2, od_b_gate2, od_o_norm_g, od_w_out,
           mlp_w1, mlp_w2):
    batch, seq, _ = x.shape
    depth = mix_norm_g.shape[0]
    tm = min(ROW_TILE, seq)
    ts = min(GLA_TILE, tm)
    tq = min(ATTN_Q_TILE, seq)
    tk = min(ATTN_SUB_TILE, tq)
    w = _prepare_weights(mix_norm_g, mlp_norm_g, final_norm_g, ev_w_in, ev_conv_w, ev_q_norm_g, ev_w_qb,
                         ev_kv_norm_g, ev_w_kvb, ev_w_out, od_w_in, od_w_gate2, od_b_gate2, od_o_norm_g,
                         od_w_out, mlp_w1, mlp_w2)
    cs = _rope_table(positions)
    xf = x.reshape(batch * seq, D_MODEL)
    for layer in range(depth):
        j = layer // 2
        if layer % 2 == 0:
            ya, q, k, v = _even_in(xf, j, layer, w, cs, seq=seq, tm=tm)
            yb = _attention(q, k, v, batch=batch, seq=seq, tq=tq, tk=tk)
            mix_a, mix_b, wo = ya, yb, w["ev_out"]
        else:
            qe, ke, klt_lo, klt_hi, v, gate, dect = _odd_in(xf, j, layer, w, tm=tm, ts=ts)
            mix_a = mix_b = _gla(qe, ke, klt_lo, klt_hi, v, gate, dect, j, w, batch=batch, seq=seq, ts=ts)
            wo = w["od_out"]
        xf = _mix_mlp(xf, mix_a, mix_b, wo, j, layer, w, tm=min(MLP_ROW_TILE, seq),
                      final_norm=(layer == depth - 1))
    return xf.reshape(batch, seq, D_MODEL)
```

```python
import functools

import jax
import jax.numpy as jnp
import numpy as np
from jax import lax
from jax.experimental import pallas as pl
from jax.experimental.pallas import tpu as pltpu

D_MODEL = 1024
CONV_W = 512
N_HEADS = 8
NOPE = 64
ROPE = 32
V_DIM = 64
Q_RANK = 384
KV_RANK = 256
ROPE_THETA = 10000.0
LANES = 128
HEAD_PAD = LANES
EVEN_IN_PAD = 3 * CONV_W + Q_RANK + KV_RANK + LANES
GLA_HEADS = 4
GLA_DK = 128
GLA_DV = 256
GLA_DQK = GLA_HEADS * GLA_DK
GLA_DVS = GLA_HEADS * GLA_DV
GATE_RANK = 16
GATE_PAD = LANES
ODD_IN_PAD = 2 * GLA_DQK + 2 * GLA_DVS + GATE_PAD
GATE_TAU = 16.0
GLA_CHUNK = 64
DEC_LANES = LANES
D_FF = 4 * D_MODEL
EPS = 1e-6

BF16 = jnp.bfloat16
F32 = jnp.float32

VMEM_LIMIT_BYTES = 56 * 1024 * 1024
ROW_TILE = 1024
GLA_TILE = 512
MLP_ROW_TILE = 1024
ATTN_Q_TILE = 2048
ATTN_KV_STEP = 512
ATTN_SUB_TILE = 512
FF_CHUNK = 512


def _rms(x, g):
    ms = jnp.mean(x * x, axis=-1, keepdims=True)
    return x * lax.rsqrt(ms + EPS) * g


def _dot(a, b):
    return jnp.dot(a, b, preferred_element_type=F32)


def _dot_nt(a, b):
    return lax.dot_general(a, b, (((1,), (1,)), ((), ())), preferred_element_type=F32)


def _const_spec(shape):
    nd = len(shape)
    return pl.BlockSpec(shape, lambda *_: (0,) * nd, pipeline_mode=pl.Buffered(1))


def _layer_spec(arr, j):
    shape = arr.shape[1:]
    nd = len(shape)
    return pl.BlockSpec((None,) + shape, lambda *_: (j,) + (0,) * nd, pipeline_mode=pl.Buffered(1))


def _params(*sem):
    return pltpu.CompilerParams(dimension_semantics=sem, vmem_limit_bytes=VMEM_LIMIT_BYTES)


def _rope(p, c, sa, sbn):
    return p * c + pltpu.roll(p, 16, 1) * sa + pltpu.roll(p, 112, 1) * sbn


def _even_in_kernel(x_ref, g_ref, win_ref, cw_ref, qg_ref, wqb_ref, wqr_ref, kvg_ref, wkb_ref, wvb_ref, vones_ref,
                    cs_ref, ya_ref, q_ref, k_ref, v_ref, carry_ref, *, tiles_per_seq, q_scale):
    tm = x_ref.shape[0]
    i = pl.program_id(0)
    h = _rms(x_ref[...], g_ref[...]).astype(BF16)

    a_c = _dot(h, win_ref[:, CONV_W:2 * CONV_W])
    a_v = _dot(h, win_ref[:, 2 * CONV_W:3 * CONV_W])
    u = a_c * a_v

    @pl.when(i % tiles_per_seq == 0)
    def _():
        carry_ref[...] = jnp.zeros_like(carry_ref)

    prev1 = carry_ref[7:8, :]
    prev2 = carry_ref[6:7, :]
    row = lax.broadcasted_iota(jnp.int32, u.shape, 0)
    u1 = jnp.where(row == 0, prev1, pltpu.roll(u, 1, 0))
    u2 = jnp.where(row == 0, prev2, jnp.where(row == 1, prev1, pltpu.roll(u, 2, 0)))
    carry_ref[...] = u[tm - 8:, :]
    conv = cw_ref[0:1, :] * u2 + cw_ref[1:2, :] * u1 + cw_ref[2:3, :] * u
    a_b = _dot(h, win_ref[:, 0:CONV_W])
    ya_ref[...] = (a_b * conv).astype(BF16)

    cst = cs_ref[...]
    cs = jnp.concatenate([cst, jnp.zeros((LANES - cst.shape[0], tm), F32)], axis=0).T
    lane = lax.broadcasted_iota(jnp.int32, cs.shape, 1)
    half = ROPE // 2
    c = jnp.where(lane < NOPE, 1.0, jnp.where(lane < NOPE + ROPE, pltpu.roll(cs, NOPE, 1), 0.0))
    sin = pltpu.roll(cs, NOPE - ROPE, 1)
    sa = jnp.where((lane >= NOPE + half) & (lane < NOPE + ROPE), sin, 0.0)
    sbn = jnp.where((lane >= NOPE) & (lane < NOPE + half), -sin, 0.0)

    zb = _dot(h, win_ref[:, 3 * CONV_W:])
    qn = _rms(zb[:, :Q_RANK], qg_ref[...]).astype(BF16)
    kvn = _rms(zb[:, Q_RANK:Q_RANK + KV_RANK], kvg_ref[...]).astype(BF16)
    k_pe = _rope(pltpu.roll(zb[:, Q_RANK + KV_RANK:], NOPE, 1), c, sa, sbn)
    qp = _dot(qn, wqb_ref[...])
    qr = _dot(qn, wqr_ref[...])
    cq = c * q_scale
    sq = jnp.where((lane >= NOPE) & (lane < NOPE + ROPE), sin, 0.0) * q_scale
    for hd in range(N_HEADS):
        sl = slice(hd * HEAD_PAD, (hd + 1) * HEAD_PAD)
        q_ref[:, sl] = (qp[:, sl] * cq + qr[:, sl] * sq).astype(BF16)
    kp = _dot(kvn, wkb_ref[...])
    for hd in range(N_HEADS):
        sl = slice(hd * HEAD_PAD, (hd + 1) * HEAD_PAD)
        k_ref[:, sl] = (kp[:, sl] + k_pe).astype(BF16)
    v_ref[...] = (_dot(kvn, wvb_ref[...]) + vones_ref[...]).astype(BF16)


def _even_in(x, j, layer, w, cs, *, seq, tm):
    t = x.shape[0]
    row = lambda width: pl.BlockSpec((tm, width), lambda i: (i, 0))
    q_scale = float((NOPE + ROPE) ** -0.5 * np.log2(np.e))
    wide = jax.ShapeDtypeStruct((t, N_HEADS * HEAD_PAD), BF16)
    return pl.pallas_call(
        functools.partial(_even_in_kernel, tiles_per_seq=seq // tm, q_scale=q_scale),
        grid=(t // tm,),
        in_specs=[row(D_MODEL), _layer_spec(w["mix_g"], layer), _layer_spec(w["ev_in"], j),
                  _layer_spec(w["ev_conv"], j), _layer_spec(w["ev_qg"], j), _layer_spec(w["ev_qb"], j),
                  _layer_spec(w["ev_qr"], j),
                  _layer_spec(w["ev_kvg"], j), _layer_spec(w["ev_kb"], j), _layer_spec(w["ev_vb"], j),
                  _const_spec(w["v_ones"].shape), pl.BlockSpec((2 * ROPE, tm), lambda i: (0, i))],
        out_specs=[row(CONV_W), row(N_HEADS * HEAD_PAD), row(N_HEADS * HEAD_PAD), row(N_HEADS * HEAD_PAD)],
        out_shape=[jax.ShapeDtypeStruct((t, CONV_W), BF16), wide, wide, wide],
        scratch_shapes=[pltpu.VMEM((8, CONV_W), F32)],
        compiler_params=_params("arbitrary"),
        name="even_in",
    )(x, w["mix_g"], w["ev_in"], w["ev_conv"], w["ev_qg"], w["ev_qb"], w["ev_qr"], w["ev_kvg"], w["ev_kb"], w["ev_vb"],
      w["v_ones"], cs)


def _attn_kernel(q_ref, k_ref, v_ref, o_ref, *, tk, kv_step, n_q):
    tq = q_ref.shape[0]
    n_sub = tq // tk
    qi = pl.program_id(2)

    def update(state, hd, r0, rows, start, width, diag):
        lanes = slice(hd * HEAD_PAD, (hd + 1) * HEAD_PAD)
        s = _dot_nt(q_ref[r0:r0 + rows, lanes], k_ref[start:start + width, lanes])
        if diag:
            row = lax.broadcasted_iota(jnp.int32, (rows, width), 0)
            col = lax.broadcasted_iota(jnp.int32, (rows, width), 1)
            s = jnp.where(col <= row + (width - rows), s, -jnp.inf)
        m_blk = jnp.max(s, axis=-1, keepdims=True)
        if state is None:
            m_new = m_blk
        else:
            m, acc = state
            m_new = jnp.maximum(m, m_blk)
        pv = _dot(jnp.exp2(s - m_new).astype(BF16), v_ref[start:start + width, lanes])
        if state is not None:
            pv = jnp.exp2(m - m_new) * acc + pv
        return m_new, pv

    def tile_body(c):
        states = [None, None]
        for start in range(0, c * tq, kv_step):
            states = [update(states[hd], hd, 0, tq, start, kv_step, False) for hd in range(2)]
        lane = lax.broadcasted_iota(jnp.int32, (tk, HEAD_PAD), 1)
        for a in range(n_sub):
            rows = slice(a * tk, (a + 1) * tk)
            accs = []
            for hd in range(2):
                st = None if states[hd] is None else (states[hd][0][rows], states[hd][1][rows])
                accs.append(update(st, hd, a * tk, tk, c * tq, (a + 1) * tk, True)[1])
            o0 = accs[0] / accs[0][:, V_DIM:V_DIM + 1]
            o1 = accs[1] / accs[1][:, 0:1]
            o_ref[rows, :] = jnp.where(lane < V_DIM, o0, o1).astype(BF16)

    for c in range(n_q):
        pl.when(qi == c)(functools.partial(tile_body, c))


def _attention(q, k, v, *, batch, seq, tq, tk):
    t = q.shape[0]
    nq = seq // tq
    return pl.pallas_call(
        functools.partial(_attn_kernel, tk=tk, kv_step=min(ATTN_KV_STEP, tq), n_q=nq),
        grid=(batch, N_HEADS // 2, nq),
        in_specs=[pl.BlockSpec((tq, 2 * HEAD_PAD), lambda b, p, i: (b * nq + i, p)),
                  pl.BlockSpec((seq, 2 * HEAD_PAD), lambda b, p, i: (b, p)),
                  pl.BlockSpec((seq, 2 * HEAD_PAD), lambda b, p, i: (b, p))],
        out_specs=pl.BlockSpec((tq, 2 * V_DIM), lambda b, p, i: (b * nq + i, p)),
        out_shape=jax.ShapeDtypeStruct((t, N_HEADS * V_DIM), BF16),
        compiler_params=_params("arbitrary", "arbitrary", "arbitrary"),
        name="attention",
    )(q, k, v)


def _chunk_cumsum(x):
    pos = lax.broadcasted_iota(jnp.int32, x.shape, 0) & (GLA_CHUNK - 1)
    d = 1
    while d < GLA_CHUNK:
        x = x + jnp.where(pos >= d, pltpu.roll(x, d, 0), 0.0)
        d *= 2
    return x


def _odd_in_kernel(x_ref, g_ref, win_ref, wg2_ref, bg2_ref,
                   qe_ref, ke_ref, klt_lo_ref, klt_hi_ref, v_ref, gate_ref, dect_ref, *, q_scale, ts):
    tm = x_ref.shape[0]
    nc = tm // GLA_CHUNK
    q0, k0, v0, g0, l0 = 0, GLA_DQK, 2 * GLA_DQK, 2 * GLA_DQK + GLA_DVS, 2 * GLA_DQK + 2 * GLA_DVS
    h = _rms(x_ref[...], g_ref[...]).astype(BF16)
    g_low = _dot(h, win_ref[:, l0:]).astype(BF16)
    gate_in = _dot(g_low, wg2_ref[...]) + bg2_ref[...]
    log_a = (jnp.minimum(gate_in, 0.0) - jnp.log1p(jnp.exp(-jnp.abs(gate_in)))) * (1.0 / GATE_TAU)
    bc = _chunk_cumsum(log_a)
    q = _dot(h, win_ref[:, q0:k0])
    qe_ref[...] = (q * q_scale * jnp.exp(bc)).astype(BF16)
    k = _dot(h, win_ref[:, k0:v0])
    ke_ref[...] = (k * jnp.exp(-bc)).astype(BF16)
    kl_parts = []
    dec_parts = []
    for ci in range(nc):
        rows = slice(ci * GLA_CHUNK, (ci + 1) * GLA_CHUNK)
        b_last = bc[(ci + 1) * GLA_CHUNK - 1:(ci + 1) * GLA_CHUNK, :]
        dec_parts.append(jnp.exp(b_last))
        kl_parts.append(k[rows, :] * jnp.exp(b_last - bc[rows, :]))
    klt = jnp.concatenate(kl_parts, axis=0).T
    odd_chunk = (lax.broadcasted_iota(jnp.int32, klt.shape, 1) & GLA_CHUNK) != 0
    klt_lo_ref[...] = jnp.where(odd_chunk, 0.0, klt).astype(BF16)
    klt_hi_ref[...] = jnp.where(odd_chunk, klt, 0.0).astype(BF16)
    per = ts // GLA_CHUNK
    for t0 in range(nc // per):
        dec_pad = jnp.concatenate(dec_parts[t0 * per:(t0 + 1) * per] + [jnp.zeros((DEC_LANES - per, GLA_DQK), F32)],
                                  axis=0)
        dect_ref[t0 * GLA_DQK:(t0 + 1) * GLA_DQK, :] = dec_pad.T
    v_ref[...] = _dot(h, win_ref[:, v0:g0]).astype(BF16)
    gate_ref[...] = _dot(h, win_ref[:, g0:l0])


def _odd_in(x, j, layer, w, *, tm, ts):
    t = x.shape[0]
    row = lambda width: pl.BlockSpec((tm, width), lambda i: (i, 0))
    col = pl.BlockSpec((GLA_DQK, tm), lambda i: (0, i))
    return pl.pallas_call(
        functools.partial(_odd_in_kernel, q_scale=float(GLA_DK ** -0.5), ts=ts),
        grid=(t // tm,),
        in_specs=[row(D_MODEL), _layer_spec(w["mix_g"], layer), _layer_spec(w["od_in"], j),
                  _layer_spec(w["od_g2"], j), _layer_spec(w["od_b2"], j)],
        out_specs=[row(GLA_DQK), row(GLA_DQK), col, col, row(GLA_DVS), row(GLA_DVS),
                   pl.BlockSpec((tm // ts * GLA_DQK, DEC_LANES), lambda i: (i, 0))],
        out_shape=[jax.ShapeDtypeStruct((t, GLA_DQK), BF16), jax.ShapeDtypeStruct((t, GLA_DQK), BF16),
                   jax.ShapeDtypeStruct((GLA_DQK, t), BF16), jax.ShapeDtypeStruct((GLA_DQK, t), BF16),
                   jax.ShapeDtypeStruct((t, GLA_DVS), BF16), jax.ShapeDtypeStruct((t, GLA_DVS), F32),
                   jax.ShapeDtypeStruct((t // ts * GLA_DQK, DEC_LANES), F32)],
        compiler_params=_params("arbitrary"),
        name="odd_in",
    )(x, w["mix_g"], w["od_in"], w["od_g2"], w["od_b2"])


def _gla_kernel(qe_ref, ke_ref, lo0_ref, lo1_ref, hi0_ref, hi1_ref, v_ref, gate_ref, dect_ref, og_ref,
                o_ref, state_ref):
    nb, ts, _ = qe_ref.shape
    pair = 2 * GLA_CHUNK
    klt_refs = ((lo0_ref, hi0_ref), (lo1_ref, hi1_ref))

    @pl.when(pl.program_id(1) == 0)
    def _():
        state_ref[...] = jnp.zeros_like(state_ref)

    r = lax.broadcasted_iota(jnp.int32, (pair, pair), 0)
    c = lax.broadcasted_iota(jnp.int32, (pair, pair), 1)
    intra = (c <= r) & ((r < GLA_CHUNK) | (c >= GLA_CHUNK))
    og = og_ref[...]
    for cp in range(ts // pair):
        rows = slice(cp * pair, (cp + 1) * pair)
        for bi in range(nb):
            heads = []
            for hd in range(GLA_HEADS):
                dk = slice(hd * GLA_DK, (hd + 1) * GLA_DK)
                dv = slice(hd * GLA_DV, (hd + 1) * GLA_DV)
                qe = qe_ref[bi, rows, dk]
                vv = v_ref[bi, rows, dv]
                a = jnp.where(intra, _dot_nt(qe, ke_ref[bi, rows, dk]), 0.0).astype(BF16)
                lhs = jnp.concatenate([a, klt_refs[bi][0][dk, rows], klt_refs[bi][1][dk, rows]], axis=0)
                res = _dot(lhs, vv)
                state = state_ref[bi, hd]
                outs = []
                for ci in range(2):
                    q_c = qe[ci * GLA_CHUNK:(ci + 1) * GLA_CHUNK]
                    outs.append(_dot(q_c, state.astype(BF16)))
                    dcol = dect_ref[bi, dk, 2 * cp + ci:2 * cp + ci + 1]
                    state = state * dcol + res[pair + ci * GLA_DK:pair + (ci + 1) * GLA_DK]
                state_ref[bi, hd] = state
                o = res[:pair] + jnp.concatenate(outs, axis=0)
                heads.append(_rms(o, og))
            gt = gate_ref[bi, rows, :]
            o_ref[bi, rows, :] = (jnp.concatenate(heads, axis=1) * (gt * jax.nn.sigmoid(gt))).astype(BF16)


def _gla(qe, ke, klt_lo, klt_hi, v, gate, dect, j, w, *, batch, seq, ts):
    nb = 2
    nt = seq // ts
    r3 = lambda a: a.reshape(batch, seq, a.shape[-1])
    blk = lambda width: pl.BlockSpec((nb, ts, width), lambda p, t: (p, t, 0))
    klt = lambda bi: pl.BlockSpec((GLA_DQK, ts), lambda p, t: (0, (nb * p + bi) * nt + t))
    out = pl.pallas_call(
        _gla_kernel,
        grid=(batch // nb, nt),
        in_specs=[blk(GLA_DQK), blk(GLA_DQK), klt(0), klt(1), klt(0), klt(1), blk(GLA_DVS), blk(GLA_DVS),
                  pl.BlockSpec((nb, GLA_DQK, DEC_LANES), lambda p, t: (p, t, 0)), _layer_spec(w["od_og"], j)],
        out_specs=blk(GLA_DVS),
        out_shape=jax.ShapeDtypeStruct((batch, seq, GLA_DVS), BF16),
        scratch_shapes=[pltpu.VMEM((nb, GLA_HEADS, GLA_DK, GLA_DV), F32)],
        compiler_params=_params("arbitrary", "arbitrary"),
        name="gla",
    )(r3(qe), r3(ke), klt_lo, klt_lo, klt_hi, klt_hi, r3(v), r3(gate),
      dect.reshape(batch, nt * GLA_DQK, DEC_LANES), w["od_og"])
    return out.reshape(batch * seq, GLA_DVS)


def _mix_mlp_kernel(x_ref, ma_ref, mb_ref, wo_ref, g_ref, w1_ref, w2_ref, fg_ref, o_ref, *, final_norm):
    y = x_ref[...] + _dot(jnp.concatenate([ma_ref[...], mb_ref[...]], axis=1), wo_ref[...])
    h = _rms(y, g_ref[...]).astype(BF16)
    acc = y
    for c0 in range(0, D_FF, FF_CHUNK):
        u = _dot(h, w1_ref[:, c0:c0 + FF_CHUNK])
        u = jnp.square(jnp.maximum(u, 0.0)).astype(BF16)
        acc = acc + _dot(u, w2_ref[c0:c0 + FF_CHUNK, :])
    if final_norm:
        acc = _rms(acc, fg_ref[...])
    o_ref[...] = acc


def _mix_mlp(x, mix_a, mix_b, wo, j, layer, w, *, tm, final_norm):
    t = x.shape[0]
    half = D_MODEL // 2
    row = lambda width: pl.BlockSpec((tm, width), lambda i: (i, 0))
    b_spec = row(half) if mix_b is not mix_a else pl.BlockSpec((tm, half), lambda i: (i, 1))
    return pl.pallas_call(
        functools.partial(_mix_mlp_kernel, final_norm=final_norm),
        grid=(t // tm,),
        in_specs=[row(D_MODEL), row(half), b_spec, _layer_spec(wo, j), _layer_spec(w["mlp_g"], layer),
                  _layer_spec(w["mlp_w1"], layer), _layer_spec(w["mlp_w2"], layer),
                  _const_spec(w["final_g"].shape)],
        out_specs=row(D_MODEL),
        out_shape=jax.ShapeDtypeStruct((t, D_MODEL), F32),
        compiler_params=_params("arbitrary"),
        name="mix_mlp",
    )(x, mix_a, mix_b, wo, w["mlp_g"], w["mlp_w1"], w["mlp_w2"], w["final_g"])


def _rope_table(positions):
    inv_freq = 1.0 / (ROPE_THETA ** (jnp.arange(0, ROPE, 2, dtype=F32) / ROPE))
    ang = inv_freq.reshape(-1, 1) * positions.astype(F32).reshape(1, -1)
    cos = jnp.cos(ang)
    sin = jnp.sin(ang)
    return jnp.concatenate([cos, cos, sin, sin], axis=0)


def _prepare_weights(mix_norm_g, mlp_norm_g, final_norm_g, ev_w_in, ev_conv_w, ev_q_norm_g, ev_w_qb,
                     ev_kv_norm_g, ev_w_kvb, ev_w_out, od_w_in, od_w_gate2, od_b_gate2, od_o_norm_g, od_w_out,
                     mlp_w1, mlp_w2):
    cast = lambda a: a.astype(BF16)
    vec = lambda a: a.reshape(a.shape[0], 1, a.shape[1])
    n_even = ev_w_in.shape[0]
    pad_last = lambda a, n: jnp.pad(a, [(0, 0)] * (a.ndim - 1) + [(0, n)])
    head_pad = HEAD_PAD - NOPE - ROPE
    wqb = pad_last(ev_w_qb.reshape(n_even, Q_RANK, N_HEADS, NOPE + ROPE), head_pad)
    half = ROPE // 2
    wqr = jnp.concatenate([jnp.zeros_like(wqb[..., :NOPE]), -wqb[..., NOPE + half:NOPE + ROPE],
                           wqb[..., NOPE:NOPE + half], jnp.zeros_like(wqb[..., NOPE + ROPE:])], axis=-1)
    wkvb = ev_w_kvb.reshape(n_even, KV_RANK, N_HEADS, NOPE + V_DIM)
    wkb = pad_last(wkvb[..., :NOPE], HEAD_PAD - NOPE)
    wv = wkvb[..., NOPE:]
    odd = (jnp.arange(N_HEADS) % 2 == 1)[None, None, :, None]
    wvb = jnp.where(odd, jnp.pad(wv, ((0, 0), (0, 0), (0, 0), (V_DIM, 0))), pad_last(wv, V_DIM))
    flat = lambda a: a.reshape(a.shape[0], a.shape[1], N_HEADS * HEAD_PAD)
    lane = np.arange(N_HEADS * HEAD_PAD) % (2 * HEAD_PAD)
    v_ones = jnp.asarray(((lane == V_DIM) | (lane == HEAD_PAD)).astype(np.float32)).reshape(1, -1)
    return {
        "mix_g": vec(mix_norm_g), "mlp_g": vec(mlp_norm_g), "final_g": final_norm_g.reshape(1, D_MODEL),
        "ev_in": cast(pad_last(ev_w_in, EVEN_IN_PAD - ev_w_in.shape[-1])), "ev_conv": ev_conv_w,
        "ev_qg": vec(ev_q_norm_g), "ev_qb": cast(flat(wqb)), "ev_qr": cast(flat(wqr)), "ev_kvg": vec(ev_kv_norm_g),
        "ev_kb": cast(flat(wkb)), "ev_vb": cast(flat(wvb)), "v_ones": v_ones, "ev_out": cast(ev_w_out),
        "od_in": cast(pad_last(od_w_in, ODD_IN_PAD - od_w_in.shape[-1])),
        "od_g2": cast(jnp.pad(od_w_gate2, ((0, 0), (0, GATE_PAD - GATE_RANK), (0, 0)))),
        "od_b2": vec(od_b_gate2), "od_og": vec(od_o_norm_g), "od_out": cast(od_w_out),
        "mlp_w1": cast(mlp_w1), "mlp_w2": cast(mlp_w2),
    }


def kernel(x, positions, mix_norm_g, mlp_norm_g, final_norm_g, ev_w_in, ev_conv_w, ev_q_norm_g, ev_w_qb,
           ev_kv_norm_g, ev_w_kvb, ev_w_out, od_w_in, od_w_gate2, od_b_gate2, od_o_norm_g, od_w_out,
           mlp_w1, mlp_w2):
    batch, seq, _ = x.shape
    depth = mix_norm_g.shape[0]
    tm = min(ROW_TILE, seq)
    ts = min(GLA_TILE, tm)
    tq = min(ATTN_Q_TILE, seq)
    tk = min(ATTN_SUB_TILE, tq)
    w = _prepare_weights(mix_norm_g, mlp_norm_g, final_norm_g, ev_w_in, ev_conv_w, ev_q_norm_g, ev_w_qb,
                         ev_kv_norm_g, ev_w_kvb, ev_w_out, od_w_in, od_w_gate2, od_b_gate2, od_o_norm_g,
                         od_w_out, mlp_w1, mlp_w2)
    cs = _rope_table(positions)
    xf = x.reshape(batch * seq, D_MODEL)
    for layer in range(depth):
        j = layer // 2
        if layer % 2 == 0:
            ya, q, k, v = _even_in(xf, j, layer, w, cs, seq=seq, tm=tm)
            yb = _attention(q, k, v, batch=batch, seq=seq, tq=tq, tk=tk)
            mix_a, mix_b, wo = ya, yb, w["ev_out"]
        else:
            qe, ke, klt_lo, klt_hi, v, gate, dect = _odd_in(xf, j, layer, w, tm=tm, ts=ts)
            mix_a = mix_b = _gla(qe, ke, klt_lo, klt_hi, v, gate, dect, j, w, batch=batch, seq=seq, ts=ts)
            wo = w["od_out"]
        xf = _mix_mlp(xf, mix_a, mix_b, wo, j, layer, w, tm=min(MLP_ROW_TILE, seq),
                      final_norm=(layer == depth - 1))
    return xf.reshape(batch, seq, D_MODEL)
```

```python
import functools

import jax
import jax.numpy as jnp
import numpy as np
from jax import lax
from jax.experimental import pallas as pl
from jax.experimental.pallas import tpu as pltpu

D_MODEL = 1024
CONV_W = 512
N_HEADS = 8
NOPE = 64
ROPE = 32
V_DIM = 64
Q_RANK = 384
KV_RANK = 256
ROPE_THETA = 10000.0
LANES = 128
HEAD_PAD = LANES
EVEN_IN_PAD = 3 * CONV_W + Q_RANK + KV_RANK + LANES
GLA_HEADS = 4
GLA_DK = 128
GLA_DV = 256
GLA_DQK = GLA_HEADS * GLA_DK
GLA_DVS = GLA_HEADS * GLA_DV
GATE_RANK = 16
GATE_PAD = LANES
ODD_IN_PAD = 2 * GLA_DQK + 2 * GLA_DVS + GATE_PAD
GATE_TAU = 16.0
GLA_CHUNK = 64
DEC_LANES = LANES
D_FF = 4 * D_MODEL
EPS = 1e-6

BF16 = jnp.bfloat16
F32 = jnp.float32

VMEM_LIMIT_BYTES = 56 * 1024 * 1024
ROW_TILE = 1024
GLA_TILE = 512
MLP_ROW_TILE = 1024
ATTN_Q_TILE = 2048
ATTN_KV_STEP = 1024
ATTN_SUB_TILE = 512
FF_CHUNK = 512


def _rms(x, g):
    ms = jnp.mean(x * x, axis=-1, keepdims=True)
    return x * lax.rsqrt(ms + EPS) * g


def _dot(a, b):
    return jnp.dot(a, b, preferred_element_type=F32)


def _dot_nt(a, b):
    return lax.dot_general(a, b, (((1,), (1,)), ((), ())), preferred_element_type=F32)


def _const_spec(shape):
    nd = len(shape)
    return pl.BlockSpec(shape, lambda *_: (0,) * nd, pipeline_mode=pl.Buffered(1))


def _layer_spec(arr, j):
    shape = arr.shape[1:]
    nd = len(shape)
    return pl.BlockSpec((None,) + shape, lambda *_: (j,) + (0,) * nd, pipeline_mode=pl.Buffered(1))


def _params(*sem):
    return pltpu.CompilerParams(dimension_semantics=sem, vmem_limit_bytes=VMEM_LIMIT_BYTES)


def _rope(p, c, sa, sbn):
    return p * c + pltpu.roll(p, 16, 1) * sa + pltpu.roll(p, 112, 1) * sbn


def _even_in_kernel(x_ref, g_ref, win_ref, cw_ref, qg_ref, wqb_ref, wqr_ref, kvg_ref, wkb_ref, wvb_ref, vones_ref,
                    cs_ref, ya_ref, q_ref, k_ref, v_ref, carry_ref, *, tiles_per_seq, q_scale):
    tm = x_ref.shape[0]
    i = pl.program_id(0)
    h = _rms(x_ref[...], g_ref[...]).astype(BF16)

    a_c = _dot(h, win_ref[:, CONV_W:2 * CONV_W])
    a_v = _dot(h, win_ref[:, 2 * CONV_W:3 * CONV_W])
    u = a_c * a_v

    @pl.when(i % tiles_per_seq == 0)
    def _():
        carry_ref[...] = jnp.zeros_like(carry_ref)

    prev1 = carry_ref[7:8, :]
    prev2 = carry_ref[6:7, :]
    row = lax.broadcasted_iota(jnp.int32, u.shape, 0)
    u1 = jnp.where(row == 0, prev1, pltpu.roll(u, 1, 0))
    u2 = jnp.where(row == 0, prev2, jnp.where(row == 1, prev1, pltpu.roll(u, 2, 0)))
    carry_ref[...] = u[tm - 8:, :]
    conv = cw_ref[0:1, :] * u2 + cw_ref[1:2, :] * u1 + cw_ref[2:3, :] * u
    a_b = _dot(h, win_ref[:, 0:CONV_W])
    ya_ref[...] = (a_b * conv).astype(BF16)

    cst = cs_ref[...]
    cs = jnp.concatenate([cst, jnp.zeros((LANES - cst.shape[0], tm), F32)], axis=0).T
    lane = lax.broadcasted_iota(jnp.int32, cs.shape, 1)
    half = ROPE // 2
    c = jnp.where(lane < NOPE, 1.0, jnp.where(lane < NOPE + ROPE, pltpu.roll(cs, NOPE, 1), 0.0))
    sin = pltpu.roll(cs, NOPE - ROPE, 1)
    sa = jnp.where((lane >= NOPE + half) & (lane < NOPE + ROPE), sin, 0.0)
    sbn = jnp.where((lane >= NOPE) & (lane < NOPE + half), -sin, 0.0)

    zb = _dot(h, win_ref[:, 3 * CONV_W:])
    qn = _rms(zb[:, :Q_RANK], qg_ref[...]).astype(BF16)
    kvn = _rms(zb[:, Q_RANK:Q_RANK + KV_RANK], kvg_ref[...]).astype(BF16)
    k_pe = _rope(pltpu.roll(zb[:, Q_RANK + KV_RANK:], NOPE, 1), c, sa, sbn)
    qp = _dot(qn, wqb_ref[...])
    qr = _dot(qn, wqr_ref[...])
    cq = c * q_scale
    sq = jnp.where((lane >= NOPE) & (lane < NOPE + ROPE), sin, 0.0) * q_scale
    for hd in range(N_HEADS):
        sl = slice(hd * HEAD_PAD, (hd + 1) * HEAD_PAD)
        q_ref[:, sl] = (qp[:, sl] * cq + qr[:, sl] * sq).astype(BF16)
    kp = _dot(kvn, wkb_ref[...])
    for hd in range(N_HEADS):
        sl = slice(hd * HEAD_PAD, (hd + 1) * HEAD_PAD)
        k_ref[:, sl] = (kp[:, sl] + k_pe).astype(BF16)
    v_ref[...] = (_dot(kvn, wvb_ref[...]) + vones_ref[...]).astype(BF16)


def _even_in(x, j, layer, w, cs, *, seq, tm):
    t = x.shape[0]
    row = lambda width: pl.BlockSpec((tm, width), lambda i: (i, 0))
    q_scale = float((NOPE + ROPE) ** -0.5 * np.log2(np.e))
    wide = jax.ShapeDtypeStruct((t, N_HEADS * HEAD_PAD), BF16)
    return pl.pallas_call(
        functools.partial(_even_in_kernel, tiles_per_seq=seq // tm, q_scale=q_scale),
        grid=(t // tm,),
        in_specs=[row(D_MODEL), _layer_spec(w["mix_g"], layer), _layer_spec(w["ev_in"], j),
                  _layer_spec(w["ev_conv"], j), _layer_spec(w["ev_qg"], j), _layer_spec(w["ev_qb"], j),
                  _layer_spec(w["ev_qr"], j),
                  _layer_spec(w["ev_kvg"], j), _layer_spec(w["ev_kb"], j), _layer_spec(w["ev_vb"], j),
                  _const_spec(w["v_ones"].shape), pl.BlockSpec((2 * ROPE, tm), lambda i: (0, i))],
        out_specs=[row(CONV_W), row(N_HEADS * HEAD_PAD), row(N_HEADS * HEAD_PAD), row(N_HEADS * HEAD_PAD)],
        out_shape=[jax.ShapeDtypeStruct((t, CONV_W), BF16), wide, wide, wide],
        scratch_shapes=[pltpu.VMEM((8, CONV_W), F32)],
        compiler_params=_params("arbitrary"),
        name="even_in",
    )(x, w["mix_g"], w["ev_in"], w["ev_conv"], w["ev_qg"], w["ev_qb"], w["ev_qr"], w["ev_kvg"], w["ev_kb"], w["ev_vb"],
      w["v_ones"], cs)


def _attn_kernel(q_ref, k_ref, v_ref, o_ref, *, tk, kv_step, n_q):
    tq = q_ref.shape[0]
    n_sub = tq // tk
    qi = pl.program_id(2)

    def update(state, hd, r0, rows, start, width, diag):
        lanes = slice(hd * HEAD_PAD, (hd + 1) * HEAD_PAD)
        s = _dot_nt(q_ref[r0:r0 + rows, lanes], k_ref[start:start + width, lanes])
        if diag:
            row = lax.broadcasted_iota(jnp.int32, (rows, width), 0)
            col = lax.broadcasted_iota(jnp.int32, (rows, width), 1)
            s = jnp.where(col <= row + (width - rows), s, -jnp.inf)
        m_blk = jnp.max(s, axis=-1, keepdims=True)
        if state is None:
            m_new = m_blk
        else:
            m, acc = state
            m_new = jnp.maximum(m, m_blk)
        pv = _dot(jnp.exp2(s - m_new).astype(BF16), v_ref[start:start + width, lanes])
        if state is not None:
            pv = jnp.exp2(m - m_new) * acc + pv
        return m_new, pv

    def tile_body(c):
        states = [None, None]
        for start in range(0, c * tq, kv_step):
            states = [update(states[hd], hd, 0, tq, start, kv_step, False) for hd in range(2)]
        lane = lax.broadcasted_iota(jnp.int32, (tk, HEAD_PAD), 1)
        for a in range(n_sub):
            rows = slice(a * tk, (a + 1) * tk)
            accs = []
            for hd in range(2):
                st = None if states[hd] is None else (states[hd][0][rows], states[hd][1][rows])
                accs.append(update(st, hd, a * tk, tk, c * tq, (a + 1) * tk, True)[1])
            o0 = accs[0] / accs[0][:, V_DIM:V_DIM + 1]
            o1 = accs[1] / accs[1][:, 0:1]
            o_ref[rows, :] = jnp.where(lane < V_DIM, o0, o1).astype(BF16)

    for c in range(n_q):
        pl.when(qi == c)(functools.partial(tile_body, c))


def _attention(q, k, v, *, batch, seq, tq, tk):
    t = q.shape[0]
    nq = seq // tq
    return pl.pallas_call(
        functools.partial(_attn_kernel, tk=tk, kv_step=min(ATTN_KV_STEP, tq), n_q=nq),
        grid=(batch, N_HEADS // 2, nq),
        in_specs=[pl.BlockSpec((tq, 2 * HEAD_PAD), lambda b, p, i: (b * nq + i, p)),
                  pl.BlockSpec((seq, 2 * HEAD_PAD), lambda b, p, i: (b, p)),
                  pl.BlockSpec((seq, 2 * HEAD_PAD), lambda b, p, i: (b, p))],
        out_specs=pl.BlockSpec((tq, 2 * V_DIM), lambda b, p, i: (b * nq + i, p)),
        out_shape=jax.ShapeDtypeStruct((t, N_HEADS * V_DIM), BF16),
        compiler_params=_params("arbitrary", "arbitrary", "arbitrary"),
        name="attention",
    )(q, k, v)


def _chunk_cumsum(x):
    pos = lax.broadcasted_iota(jnp.int32, x.shape, 0) & (GLA_CHUNK - 1)
    d = 1
    while d < GLA_CHUNK:
        x = x + jnp.where(pos >= d, pltpu.roll(x, d, 0), 0.0)
        d *= 2
    return x


def _odd_in_kernel(x_ref, g_ref, win_ref, wg2_ref, bg2_ref,
                   qe_ref, ke_ref, klt_ref, v_ref, gate_ref, dect_ref, *, q_scale, ts):
    tm = x_ref.shape[0]
    nc = tm // GLA_CHUNK
    q0, k0, v0, g0, l0 = 0, GLA_DQK, 2 * GLA_DQK, 2 * GLA_DQK + GLA_DVS, 2 * GLA_DQK + 2 * GLA_DVS
    h = _rms(x_ref[...], g_ref[...]).astype(BF16)
    g_low = _dot(h, win_ref[:, l0:]).astype(BF16)
    gate_in = _dot(g_low, wg2_ref[...]) + bg2_ref[...]
    log_a = (jnp.minimum(gate_in, 0.0) - jnp.log1p(jnp.exp(-jnp.abs(gate_in)))) * (1.0 / GATE_TAU)
    bc = _chunk_cumsum(log_a)
    q = _dot(h, win_ref[:, q0:k0])
    qe_ref[...] = (q * q_scale * jnp.exp(bc)).astype(BF16)
    k = _dot(h, win_ref[:, k0:v0])
    ke_ref[...] = (k * jnp.exp(-bc)).astype(BF16)
    kl_parts = []
    dec_parts = []
    for ci in range(nc):
        rows = slice(ci * GLA_CHUNK, (ci + 1) * GLA_CHUNK)
        b_last = bc[(ci + 1) * GLA_CHUNK - 1:(ci + 1) * GLA_CHUNK, :]
        dec_parts.append(jnp.exp(b_last))
        kl_parts.append(k[rows, :] * jnp.exp(b_last - bc[rows, :]))
    klt_ref[...] = jnp.concatenate(kl_parts, axis=0).T.astype(BF16)
    per = ts // GLA_CHUNK
    for t0 in range(nc // per):
        dec_pad = jnp.concatenate(dec_parts[t0 * per:(t0 + 1) * per] + [jnp.zeros((DEC_LANES - per, GLA_DQK), F32)],
                                  axis=0)
        dect_ref[t0 * GLA_DQK:(t0 + 1) * GLA_DQK, :] = dec_pad.T
    v_ref[...] = _dot(h, win_ref[:, v0:g0]).astype(BF16)
    gate_ref[...] = _dot(h, win_ref[:, g0:l0])


def _odd_in(x, j, layer, w, *, tm, ts):
    t = x.shape[0]
    row = lambda width: pl.BlockSpec((tm, width), lambda i: (i, 0))
    col = pl.BlockSpec((GLA_DQK, tm), lambda i: (0, i))
    return pl.pallas_call(
        functools.partial(_odd_in_kernel, q_scale=float(GLA_DK ** -0.5), ts=ts),
        grid=(t // tm,),
        in_specs=[row(D_MODEL), _layer_spec(w["mix_g"], layer), _layer_spec(w["od_in"], j),
                  _layer_spec(w["od_g2"], j), _layer_spec(w["od_b2"], j)],
        out_specs=[row(GLA_DQK), row(GLA_DQK), col, row(GLA_DVS), row(GLA_DVS),
                   pl.BlockSpec((tm // ts * GLA_DQK, DEC_LANES), lambda i: (i, 0))],
        out_shape=[jax.ShapeDtypeStruct((t, GLA_DQK), BF16), jax.ShapeDtypeStruct((t, GLA_DQK), BF16),
                   jax.ShapeDtypeStruct((GLA_DQK, t), BF16),
                   jax.ShapeDtypeStruct((t, GLA_DVS), BF16), jax.ShapeDtypeStruct((t, GLA_DVS), F32),
                   jax.ShapeDtypeStruct((t // ts * GLA_DQK, DEC_LANES), F32)],
        compiler_params=_params("arbitrary"),
        name="odd_in",
    )(x, w["mix_g"], w["od_in"], w["od_g2"], w["od_b2"])


def _gla_kernel(qe_ref, ke_ref, kl0_ref, kl1_ref, v_ref, gate_ref, dect_ref, og_ref, o_ref, state_ref):
    nb, ts, _ = qe_ref.shape
    pair = 2 * GLA_CHUNK
    klt_refs = (kl0_ref, kl1_ref)

    @pl.when(pl.program_id(1) == 0)
    def _():
        state_ref[...] = jnp.zeros_like(state_ref)

    r = lax.broadcasted_iota(jnp.int32, (pair, pair), 0)
    c = lax.broadcasted_iota(jnp.int32, (pair, pair), 1)
    intra = (c <= r) & ((r < GLA_CHUNK) | (c >= GLA_CHUNK))
    og = og_ref[...]
    for cp in range(ts // pair):
        rows = slice(cp * pair, (cp + 1) * pair)
        for bi in range(nb):
            heads = []
            for hd in range(GLA_HEADS):
                dk = slice(hd * GLA_DK, (hd + 1) * GLA_DK)
                dv = slice(hd * GLA_DV, (hd + 1) * GLA_DV)
                qe = qe_ref[bi, rows, dk]
                vv = v_ref[bi, rows, dv]
                a = jnp.where(intra, _dot_nt(qe, ke_ref[bi, rows, dk]), 0.0).astype(BF16)
                klt = klt_refs[bi][dk, rows]
                zero = jnp.zeros_like(klt)
                lhs = jnp.concatenate([a, jnp.where(c < GLA_CHUNK, klt, zero), jnp.where(c >= GLA_CHUNK, klt, zero)],
                                      axis=0)
                res = _dot(lhs, vv)
                state = state_ref[bi, hd]
                outs = []
                for ci in range(2):
                    q_c = qe[ci * GLA_CHUNK:(ci + 1) * GLA_CHUNK]
                    outs.append(_dot(q_c, state.astype(BF16)))
                    dcol = dect_ref[bi, dk, 2 * cp + ci:2 * cp + ci + 1]
                    state = state * dcol + res[pair + ci * GLA_DK:pair + (ci + 1) * GLA_DK]
                state_ref[bi, hd] = state
                o = res[:pair] + jnp.concatenate(outs, axis=0)
                heads.append(_rms(o, og))
            gt = gate_ref[bi, rows, :]
            o_ref[bi, rows, :] = (jnp.concatenate(heads, axis=1) * (gt * jax.nn.sigmoid(gt))).astype(BF16)


def _gla(qe, ke, klt_t, v, gate, dect, j, w, *, batch, seq, ts):
    nb = 2
    nt = seq // ts
    r3 = lambda a: a.reshape(batch, seq, a.shape[-1])
    blk = lambda width: pl.BlockSpec((nb, ts, width), lambda p, t: (p, t, 0))
    klt = lambda bi: pl.BlockSpec((GLA_DQK, ts), lambda p, t: (0, (nb * p + bi) * nt + t))
    out = pl.pallas_call(
        _gla_kernel,
        grid=(batch // nb, nt),
        in_specs=[blk(GLA_DQK), blk(GLA_DQK), klt(0), klt(1), blk(GLA_DVS), blk(GLA_DVS),
                  pl.BlockSpec((nb, GLA_DQK, DEC_LANES), lambda p, t: (p, t, 0)), _layer_spec(w["od_og"], j)],
        out_specs=blk(GLA_DVS),
        out_shape=jax.ShapeDtypeStruct((batch, seq, GLA_DVS), BF16),
        scratch_shapes=[pltpu.VMEM((nb, GLA_HEADS, GLA_DK, GLA_DV), F32)],
        compiler_params=_params("arbitrary", "arbitrary"),
        name="gla",
    )(r3(qe), r3(ke), klt_t, klt_t, r3(v), r3(gate),
      dect.reshape(batch, nt * GLA_DQK, DEC_LANES), w["od_og"])
    return out.reshape(batch * seq, GLA_DVS)


def _mix_mlp_kernel(x_ref, ma_ref, mb_ref, wo_ref, g_ref, w1_ref, w2_ref, fg_ref, o_ref, *, final_norm):
    y = x_ref[...] + _dot(jnp.concatenate([ma_ref[...], mb_ref[...]], axis=1), wo_ref[...])
    h = _rms(y, g_ref[...]).astype(BF16)
    acc = y
    for c0 in range(0, D_FF, FF_CHUNK):
        u = _dot(h, w1_ref[:, c0:c0 + FF_CHUNK])
        u = jnp.square(jnp.maximum(u, 0.0)).astype(BF16)
        acc = acc + _dot(u, w2_ref[c0:c0 + FF_CHUNK, :])
    if final_norm:
        acc = _rms(acc, fg_ref[...])
    o_ref[...] = acc


def _mix_mlp(x, mix_a, mix_b, wo, j, layer, w, *, tm, final_norm):
    t = x.shape[0]
    half = D_MODEL // 2
    row = lambda width: pl.BlockSpec((tm, width), lambda i: (i, 0))
    b_spec = row(half) if mix_b is not mix_a else pl.BlockSpec((tm, half), lambda i: (i, 1))
    return pl.pallas_call(
        functools.partial(_mix_mlp_kernel, final_norm=final_norm),
        grid=(t // tm,),
        in_specs=[row(D_MODEL), row(half), b_spec, _layer_spec(wo, j), _layer_spec(w["mlp_g"], layer),
                  _layer_spec(w["mlp_w1"], layer), _layer_spec(w["mlp_w2"], layer),
                  _const_spec(w["final_g"].shape)],
        out_specs=row(D_MODEL),
        out_shape=jax.ShapeDtypeStruct((t, D_MODEL), F32),
        compiler_params=_params("arbitrary"),
        name="mix_mlp",
    )(x, mix_a, mix_b, wo, w["mlp_g"], w["mlp_w1"], w["mlp_w2"], w["final_g"])


def _rope_table(positions):
    inv_freq = 1.0 / (ROPE_THETA ** (jnp.arange(0, ROPE, 2, dtype=F32) / ROPE))
    ang = inv_freq.reshape(-1, 1) * positions.astype(F32).reshape(1, -1)
    cos = jnp.cos(ang)
    sin = jnp.sin(ang)
    return jnp.concatenate([cos, cos, sin, sin], axis=0)


def _prepare_weights(mix_norm_g, mlp_norm_g, final_norm_g, ev_w_in, ev_conv_w, ev_q_norm_g, ev_w_qb,
                     ev_kv_norm_g, ev_w_kvb, ev_w_out, od_w_in, od_w_gate2, od_b_gate2, od_o_norm_g, od_w_out,
                     mlp_w1, mlp_w2):
    cast = lambda a: a.astype(BF16)
    vec = lambda a: a.reshape(a.shape[0], 1, a.shape[1])
    n_even = ev_w_in.shape[0]
    pad_last = lambda a, n: jnp.pad(a, [(0, 0)] * (a.ndim - 1) + [(0, n)])
    head_pad = HEAD_PAD - NOPE - ROPE
    wqb = pad_last(ev_w_qb.reshape(n_even, Q_RANK, N_HEADS, NOPE + ROPE), head_pad)
    half = ROPE // 2
    wqr = jnp.concatenate([jnp.zeros_like(wqb[..., :NOPE]), -wqb[..., NOPE + half:NOPE + ROPE],
                           wqb[..., NOPE:NOPE + half], jnp.zeros_like(wqb[..., NOPE + ROPE:])], axis=-1)
    wkvb = ev_w_kvb.reshape(n_even, KV_RANK, N_HEADS, NOPE + V_DIM)
    wkb = pad_last(wkvb[..., :NOPE], HEAD_PAD - NOPE)
    wv = wkvb[..., NOPE:]
    odd = (jnp.arange(N_HEADS) % 2 == 1)[None, None, :, None]
    wvb = jnp.where(odd, jnp.pad(wv, ((0, 0), (0, 0), (0, 0), (V_DIM, 0))), pad_last(wv, V_DIM))
    flat = lambda a: a.reshape(a.shape[0], a.shape[1], N_HEADS * HEAD_PAD)
    lane = np.arange(N_HEADS * HEAD_PAD) % (2 * HEAD_PAD)
    v_ones = jnp.asarray(((lane == V_DIM) | (lane == HEAD_PAD)).astype(np.float32)).reshape(1, -1)
    return {
        "mix_g": vec(mix_norm_g), "mlp_g": vec(mlp_norm_g), "final_g": final_norm_g.reshape(1, D_MODEL),
        "ev_in": cast(pad_last(ev_w_in, EVEN_IN_PAD - ev_w_in.shape[-1])), "ev_conv": ev_conv_w,
        "ev_qg": vec(ev_q_norm_g), "ev_qb": cast(flat(wqb)), "ev_qr": cast(flat(wqr)), "ev_kvg": vec(ev_kv_norm_g),
        "ev_kb": cast(flat(wkb)), "ev_vb": cast(flat(wvb)), "v_ones": v_ones, "ev_out": cast(ev_w_out),
        "od_in": cast(pad_last(od_w_in, ODD_IN_PAD - od_w_in.shape[-1])),
        "od_g2": cast(jnp.pad(od_w_gate2, ((0, 0), (0, GATE_PAD - GATE_RANK), (0, 0)))),
        "od_b2": vec(od_b_gate2), "od_og": vec(od_o_norm_g), "od_out": cast(od_w_out),
        "mlp_w1": cast(mlp_w1), "mlp_w2": cast(mlp_w2),
    }


def kernel(x, positions, mix_norm_g, mlp_norm_g, final_norm_g, ev_w_in, ev_conv_w, ev_q_norm_g, ev_w_qb,
           ev_kv_norm_g, ev_w_kvb, ev_w_out, od_w_in, od_w_gate2, od_b_gate2, od_o_norm_g, od_w_out,
           mlp_w1, mlp_w2):
    batch, seq, _ = x.shape
    depth = mix_norm_g.shape[0]
    tm = min(ROW_TILE, seq)
    ts = min(GLA_TILE, tm)
    tq = min(ATTN_Q_TILE, seq)
    tk = min(ATTN_SUB_TILE, tq)
    w = _prepare_weights(mix_norm_g, mlp_norm_g, final_norm_g, ev_w_in, ev_conv_w, ev_q_norm_g, ev_w_qb,
                         ev_kv_norm_g, ev_w_kvb, ev_w_out, od_w_in, od_w_gate2, od_b_gate2, od_o_norm_g,
                         od_w_out, mlp_w1, mlp_w2)
    cs = _rope_table(positions)
    xf = x.reshape(batch * seq, D_MODEL)
    for layer in range(depth):
        j = layer // 2
        if layer % 2 == 0:
            ya, q, k, v = _even_in(xf, j, layer, w, cs, seq=seq, tm=tm)
            yb = _attention(q, k, v, batch=batch, seq=seq, tq=tq, tk=tk)
            mix_a, mix_b, wo = ya, yb, w["ev_out"]
        else:
            qe, ke, klt_t, v, gate, dect = _odd_in(xf, j, layer, w, tm=tm, ts=ts)
            mix_a = mix_b = _gla(qe, ke, klt_t, v, gate, dect, j, w, batch=batch, seq=seq, ts=ts)
            wo = w["od_out"]
        xf = _mix_mlp(xf, mix_a, mix_b, wo, j, layer, w, tm=min(MLP_ROW_TILE, seq),
                      final_norm=(layer == depth - 1))
    return xf.reshape(batch, seq, D_MODEL)
```
